```python
import jax, jax.numpy as jnp
from jax import lax
import numpy as np

D_MODEL = 2048
BATCH = 2
SEQ = 8192
DEPTH = 2

CHUNK = 64
N_MEM = 256
GLA_HEADS = 4
GLA_DK = D_MODEL // 2
GLA_DV = D_MODEL
GLA_HK = GLA_DK // GLA_HEADS
GLA_HV = GLA_DV // GLA_HEADS
GLA_RANK = 16
GLA_TAU = 16.0
CONV_WIDTH = D_MODEL
CONV_K = 31
XA_HEADS = 4
XA_HEAD_DIM = 128
XA_WIDTH = XA_HEADS * XA_HEAD_DIM
D_FF = 5632
FFN_RES = 0.5
N_BRANCH = 2
EPS = 1e-6
SPLITS = (GLA_DK, GLA_DK, GLA_DV, GLA_RANK, GLA_DV, 2 * CONV_WIDTH, N_BRANCH * D_MODEL)
D_IN = 2 * GLA_DK + 2 * GLA_DV + GLA_RANK + 2 * CONV_WIDTH + N_BRANCH * D_MODEL

kernel_name = "gla_conformer_gated_hybrid"


def rmsnorm(x, g):
    xf = x.astype(jnp.float32)
    y = xf * lax.rsqrt(jnp.mean(xf * xf, axis=-1, keepdims=True) + EPS)
    return (y * g.astype(jnp.float32)).astype(x.dtype)


def layernorm(x, g, b):
    xf = x.astype(jnp.float32)
    mu = jnp.mean(xf, axis=-1, keepdims=True)
    xc = xf - mu
    y = xc * lax.rsqrt(jnp.mean(xc * xc, axis=-1, keepdims=True) + EPS)
    return (y * g.astype(jnp.float32) + b.astype(jnp.float32)).astype(x.dtype)


def swiglu_half(x, norm_g, w_in, w_out):
    h = rmsnorm(x, norm_g)
    a, b = jnp.split(h @ w_in, 2, axis=-1)
    return x + FFN_RES * ((jax.nn.silu(a) * b) @ w_out)


def gla_chunked(q, k, v, log_a):
    B, T, H, dk = q.shape
    dv = v.shape[-1]
    n = T // CHUNK

    def to_chunks(t):
        return t.reshape(B, n, CHUNK, H, t.shape[-1]).transpose(1, 0, 3, 2, 4)

    mask = jnp.tril(jnp.ones((CHUNK, CHUNK), dtype=bool))[:, :, None]

    def step(S, inp):
        qi, ki, vi, ai = inp
        qf = qi.astype(jnp.float32)
        kf = ki.astype(jnp.float32)
        vf = vi.astype(jnp.float32)
        b = jnp.cumsum(ai.astype(jnp.float32), axis=2)
        o_inter = jnp.einsum('bhcd,bhde->bhce', qf * jnp.exp(b), S)
        diff = b[:, :, :, None, :] - b[:, :, None, :, :]
        decay = jnp.exp(jnp.where(mask, diff, -jnp.inf))
        scores = jnp.einsum('bhid,bhjd,bhijd->bhij', qf, kf, decay)
        o = o_inter + jnp.einsum('bhij,bhje->bhie', scores, vf)
        b_last = b[:, :, -1:, :]
        S_new = (jnp.exp(b_last[:, :, 0, :])[..., None] * S
                 + jnp.einsum('bhcd,bhce->bhde', kf * jnp.exp(b_last - b), vf))
        return S_new, o

    S0 = jnp.zeros((B, H, dk, dv), jnp.float32)
    _, o = lax.scan(step, S0, (to_chunks(q), to_chunks(k), to_chunks(v), to_chunks(log_a)))
    return o.transpose(1, 0, 3, 2, 4).reshape(B, T, H, dv)


def mixing(x, norm_g, w_in, gate_w2, gate_b, gla_norm_g, gla_proj,
           conv_w, conv_b, conv_ln_g, conv_ln_b, conv_proj, branch_gate_b, w_out):
    B, T, _ = x.shape
    h = rmsnorm(x, norm_g)
    z = h @ w_in
    q, k, v, r, g, u, gts = jnp.split(z, np.cumsum(SPLITS)[:-1].tolist(), axis=-1)
    log_a = jax.nn.log_sigmoid((r @ gate_w2 + gate_b).astype(jnp.float32)) / GLA_TAU
    q = q.reshape(B, T, GLA_HEADS, GLA_HK) * (GLA_HK ** -0.5)
    k = k.reshape(B, T, GLA_HEADS, GLA_HK)
    v = v.reshape(B, T, GLA_HEADS, GLA_HV)
    log_a = log_a.reshape(B, T, GLA_HEADS, GLA_HK)
    o = gla_chunked(q, k, v, log_a)
    o = o * lax.rsqrt(jnp.mean(o * o, axis=-1, keepdims=True) + EPS)
    o = (o.reshape(B, T, GLA_DV) * gla_norm_g.astype(jnp.float32)).astype(x.dtype)
    y_a = (o * jax.nn.silu(g)) @ gla_proj
    ua, ug = jnp.split(u, 2, axis=-1)
    c = ua * jax.nn.sigmoid(ug)
    c = lax.conv_general_dilated(
        c, conv_w[:, None, :], window_strides=(1,), padding=[(CONV_K - 1, 0)],
        dimension_numbers=('NWC', 'WIO', 'NWC'), feature_group_count=CONV_WIDTH) + conv_b
    c = jax.nn.silu(layernorm(c, conv_ln_g, conv_ln_b))
    y_b = c @ conv_proj
    gates = jax.nn.sigmoid(gts + branch_gate_b).reshape(B, T, N_BRANCH, D_MODEL)
    m = gates[:, :, 0, :] * y_a + gates[:, :, 1, :] * y_b
    return x + m @ w_out


def memory_xattn(x, mem, norm_g, mem_norm_g, w_q, w_kv, w_out):
    B, T, _ = x.shape
    h = rmsnorm(x, norm_g)
    mh = rmsnorm(mem, mem_norm_g)
    q = (h @ w_q).reshape(B, T, XA_HEADS, XA_HEAD_DIM)
    k, v = jnp.split(mh @ w_kv, 2, axis=-1)
    k = k.reshape(B, N_MEM, XA_HEADS, XA_HEAD_DIM)
    v = v.reshape(B, N_MEM, XA_HEADS, XA_HEAD_DIM)
    s = jnp.einsum('bthd,bmhd->bhtm', q, k).astype(jnp.float32) * (XA_HEAD_DIM ** -0.5)
    p = jax.nn.softmax(s, axis=-1).astype(x.dtype)
    o = jnp.einsum('bhtm,bmhd->bthd', p, v).reshape(B, T, XA_WIDTH)
    return x + o @ w_out


def setup_inputs(seed: int = 0) -> dict:
    key = jax.random.key(seed)
    ks = iter(jax.random.split(key, 32))
    f32 = jnp.float32
    L, D = DEPTH, D_MODEL

    def w(shape, fan_in):
        return jax.random.normal(next(ks), shape, f32) * (fan_in ** -0.5)

    def gain(shape):
        return 1.0 + 0.01 * jax.random.normal(next(ks), shape, f32)

    def bias(shape, s=0.01):
        return s * jax.random.normal(next(ks), shape, f32)

    return {
        "x": jax.random.normal(next(ks), (BATCH, SEQ, D), f32),
        "mem": jax.random.normal(next(ks), (BATCH, N_MEM, D), f32),
        "ffn1_norm": gain((L, D)),
        "ffn1_w_in": w((L, D, 2 * D_FF), D),
        "ffn1_w_out": w((L, D_FF, D), D_FF),
        "mix_norm": gain((L, D)),
        "mix_w_in": w((L, D, D_IN), D),
        "gla_gate_w2": w((L, GLA_RANK, GLA_DK), GLA_RANK),
        "gla_gate_b": bias((L, GLA_DK), 0.1),
        "gla_out_norm": gain((L, GLA_DV)),
        "gla_proj": w((L, GLA_DV, D), GLA_DV),
        "conv_w": w((L, CONV_K, CONV_WIDTH), CONV_K),
        "conv_b": bias((L, CONV_WIDTH)),
        "conv_ln_g": gain((L, CONV_WIDTH)),
        "conv_ln_b": bias((L, CONV_WIDTH)),
        "conv_proj": w((L, CONV_WIDTH, D), CONV_WIDTH),
        "branch_gate_b": bias((L, N_BRANCH * D)),
        "mix_w_out": w((L, D, D), D),
        "xa_norm": gain((L, D)),
        "xa_mem_norm": gain((L, D)),
        "xa_w_q": w((L, D, XA_WIDTH), D),
        "xa_w_kv": w((L, D, 2 * XA_WIDTH), D),
        "xa_w_out": w((L, XA_WIDTH, D), XA_WIDTH),
        "ffn2_norm": gain((L, D)),
        "ffn2_w_in": w((L, D, 2 * D_FF), D),
        "ffn2_w_out": w((L, D_FF, D), D_FF),
        "final_norm": gain((D,)),
    }


def reference(x, mem, ffn1_norm, ffn1_w_in, ffn1_w_out, mix_norm, mix_w_in, gla_gate_w2,
              gla_gate_b, gla_out_norm, gla_proj, conv_w, conv_b, conv_ln_g, conv_ln_b,
              conv_proj, branch_gate_b, mix_w_out, xa_norm, xa_mem_norm, xa_w_q, xa_w_kv,
              xa_w_out, ffn2_norm, ffn2_w_in, ffn2_w_out, final_norm):
    for l in range(DEPTH):
        x = swiglu_half(x, ffn1_norm[l], ffn1_w_in[l], ffn1_w_out[l])
        x = mixing(x, mix_norm[l], mix_w_in[l], gla_gate_w2[l], gla_gate_b[l], gla_out_norm[l],
                   gla_proj[l], conv_w[l], conv_b[l], conv_ln_g[l], conv_ln_b[l], conv_proj[l],
                   branch_gate_b[l], mix_w_out[l])
        x = memory_xattn(x, mem, xa_norm[l], xa_mem_norm[l], xa_w_q[l], xa_w_kv[l], xa_w_out[l])
        x = swiglu_half(x, ffn2_norm[l], ffn2_w_in[l], ffn2_w_out[l])
    return rmsnorm(x, final_norm)
```

```python
import functools

import jax
import jax.numpy as jnp
from jax import lax
from jax.experimental import pallas as pl
from jax.experimental.pallas import tpu as pltpu

F32 = jnp.float32
BF16 = jnp.bfloat16

D_MODEL = 2048
CHUNK = 64
SUB = 16
GLA_HEADS = 4
GLA_DK = D_MODEL // 2
GLA_DV = D_MODEL
GLA_HK = GLA_DK // GLA_HEADS
GLA_HV = GLA_DV // GLA_HEADS
GLA_RANK = 16
GLA_TAU = 16.0
CONV_K = 31
CONV_HALO = 32
XA_HEADS = 4
XA_HEAD_DIM = 128
XA_WIDTH = XA_HEADS * XA_HEAD_DIM
D_FF = 5632
FFN_RES = 0.5
EPS = 1e-6
LANE = 128
R_PAD = LANE
Z_Q, Z_K, Z_V, Z_G, Z_UA, Z_UG, Z_GT0, Z_GT1 = 0, 1024, 2048, 4096, 6144, 8192, 10240, 12288
Z_WIDTH = 14336

VMEM_LIMIT = 56 * 1024 * 1024


def _params(sem):
    return pltpu.CompilerParams(dimension_semantics=sem, vmem_limit_bytes=VMEM_LIMIT)


def _rms(x, g):
    return x * lax.rsqrt(jnp.mean(x * x, axis=-1, keepdims=True) + EPS) * g


def _silu(x):
    return x * jax.nn.sigmoid(x)


def _dot(a, b):
    return jnp.dot(a, b, preferred_element_type=F32)


def _dot_nt(a, b):
    return lax.dot_general(a, b, (((1,), (1,)), ((), ())), preferred_element_type=F32)


def _dot_tn(a, b):
    return lax.dot_general(a, b, (((0,), (0,)), ((), ())), preferred_element_type=F32)


def _ffn_kernel(x_ref, g_ref, wa_ref, wb_ref, w2_ref, fg_ref, o_ref, h_ref, *, final_norm):
    j = pl.program_id(1)

    @pl.when(j == 0)
    def _():
        x = x_ref[...]
        h_ref[...] = _rms(x, g_ref[...]).astype(BF16)
        o_ref[...] = x

    h = h_ref[...]
    a = _dot(h, wa_ref[...])
    b = _dot(h, wb_ref[...])
    g = (_silu(a) * b * FFN_RES).astype(BF16)
    o_ref[...] += _dot(g, w2_ref[...])

    if final_norm:
        @pl.when(j == pl.num_programs(1) - 1)
        def _():
            o_ref[...] = _rms(o_ref[...], fg_ref[...])


def _ffn(x, norm_g, w_in, w_out, final_g, *, final_norm, tm=512, tf=512):
    n = x.shape[0]
    nf = D_FF // tf
    return pl.pallas_call(
        functools.partial(_ffn_kernel, final_norm=final_norm),
        out_shape=jax.ShapeDtypeStruct((n, D_MODEL), F32),
        grid=(n // tm, nf),
        in_specs=[
            pl.BlockSpec((tm, D_MODEL), lambda i, j: (i, 0)),
            pl.BlockSpec((1, D_MODEL), lambda i, j: (0, 0)),
            pl.BlockSpec((D_MODEL, tf), lambda i, j: (0, j)),
            pl.BlockSpec((D_MODEL, tf), lambda i, j: (0, j + nf)),
            pl.BlockSpec((tf, D_MODEL), lambda i, j: (j, 0)),
            pl.BlockSpec((1, D_MODEL), lambda i, j: (0, 0)),
        ],
        out_specs=pl.BlockSpec((tm, D_MODEL), lambda i, j: (i, 0)),
        scratch_shapes=[pltpu.VMEM((tm, D_MODEL), BF16)],
        compiler_params=_params(("parallel", "arbitrary")),
        name="ffn",
    )(x, norm_g, w_in, w_in, w_out, final_g)


def _norm_proj_kernel(x_ref, g_ref, w_ref, wr_ref, o_ref, r_ref, h_ref):
    @pl.when(pl.program_id(1) == 0)
    def _():
        h = _rms(x_ref[...], g_ref[...]).astype(BF16)
        h_ref[...] = h
        r_ref[...] = _dot(h, wr_ref[...])

    o_ref[...] = _dot(h_ref[...], w_ref[...])


def _norm_proj(x, norm_g, w, w_small, *, tm, tn):
    n = x.shape[0]
    nout = w.shape[1]
    ns = w_small.shape[1]
    return pl.pallas_call(
        _norm_proj_kernel,
        out_shape=(jax.ShapeDtypeStruct((n, nout), F32), jax.ShapeDtypeStruct((n, ns), F32)),
        grid=(n // tm, nout // tn),
        in_specs=[
            pl.BlockSpec((tm, D_MODEL), lambda i, j: (i, 0)),
            pl.BlockSpec((1, D_MODEL), lambda i, j: (0, 0)),
            pl.BlockSpec((D_MODEL, tn), lambda i, j: (0, j)),
            pl.BlockSpec((D_MODEL, ns), lambda i, j: (0, 0)),
        ],
        out_specs=(pl.BlockSpec((tm, tn), lambda i, j: (i, j)),
                   pl.BlockSpec((tm, ns), lambda i, j: (i, 0))),
        scratch_shapes=[pltpu.VMEM((tm, D_MODEL), BF16)],
        compiler_params=_params(("parallel", "arbitrary")),
        name="norm_proj",
    )(x, norm_g, w, w_small)


def _norm_matmul_kernel(x_ref, g_ref, w_ref, o_ref):
    h = _rms(x_ref[...], g_ref[...]).astype(BF16)
    o_ref[...] = _dot(h, w_ref[...])


def _norm_matmul(x, norm_g, w, *, tm):
    n = x.shape[0]
    nout = w.shape[1]
    return pl.pallas_call(
        _norm_matmul_kernel,
        out_shape=jax.ShapeDtypeStruct((n, nout), F32),
        grid=(n // tm,),
        in_specs=[
            pl.BlockSpec((tm, D_MODEL), lambda i: (i, 0)),
            pl.BlockSpec((1, D_MODEL), lambda i: (0, 0)),
            pl.BlockSpec((D_MODEL, nout), lambda i: (0, 0)),
        ],
        out_specs=pl.BlockSpec((tm, nout), lambda i: (i, 0)),
        compiler_params=_params(("parallel",)),
        name="norm_matmul",
    )(x, norm_g, w)


def _matmul_kernel(x_ref, w_ref, o_ref):
    o_ref[...] = _dot(x_ref[...], w_ref[...])


def _matmul(x, w, *, tm):
    n, k = x.shape
    nout = w.shape[1]
    return pl.pallas_call(
        _matmul_kernel,
        out_shape=jax.ShapeDtypeStruct((n, nout), F32),
        grid=(n // tm,),
        in_specs=[pl.BlockSpec((tm, k), lambda i: (i, 0)),
                  pl.BlockSpec((k, nout), lambda i: (0, 0))],
        out_specs=pl.BlockSpec((tm, nout), lambda i: (i, 0)),
        compiler_params=_params(("parallel",)),
        name="matmul",
    )(x, w)


def _split3(x):
    hi = x.astype(BF16)
    r1 = x - hi.astype(F32)
    mid = r1.astype(BF16)
    lo = (r1 - mid.astype(F32)).astype(BF16)
    return hi, mid, lo


def _gla_kernel(q_ref, k_ref, v_ref, g_ref, r_ref, w2_ref, gb_ref, gn_ref, o_ref, st_ref, la_ref,
                *, t_blk):
    @pl.when(pl.program_id(2) == 0)
    def _():
        st_ref[...] = jnp.zeros_like(st_ref)

    pre = _dot(r_ref[...].astype(BF16), w2_ref[...]) + gb_ref[...]
    la_ref[...] = (jnp.minimum(pre, 0.0) - jnp.log1p(jnp.exp(-jnp.abs(pre)))) * (1.0 / GLA_TAU)

    row = lax.broadcasted_iota(jnp.int32, (CHUNK, CHUNK), 0)
    col = lax.broadcasted_iota(jnp.int32, (CHUNK, CHUNK), 1)
    tri = (row >= col).astype(BF16)
    row_blk = row // SUB
    col_blk = col // SUB
    off_mask = col_blk < row_blk
    diag_mask = (col_blk == row_blk) & (col <= row)
    col_rel = col - row_blk * SUB
    n_sub = CHUNK // SUB

    for c in range(t_blk // CHUNK):
        rows = pl.ds(c * CHUNK, CHUNK)
        q = q_ref[rows, :] * (GLA_HK ** -0.5)
        k = k_ref[rows, :]
        v = v_ref[rows, :].astype(BF16)
        la = la_ref[rows, :]
        hi, mid, lo = _split3(la)
        b = _dot(tri, hi) + _dot(tri, mid) + _dot(tri, lo)
        b_last = b[CHUNK - 1:CHUNK, :]

        st = st_ref[...]
        o = _dot_nt((q * jnp.exp(b)).astype(BF16), st.astype(BF16))
        kd = (k * jnp.exp(b_last - b)).astype(BF16)
        st_ref[...] = st * jnp.exp(b_last) + _dot_tn(v, kd)

        parts = [jnp.zeros((SUB, CHUNK), F32)]
        for i in range(1, n_sub):
            b_ref = b[i * SUB - 1:i * SUB, :]
            rs = slice(i * SUB, (i + 1) * SUB)
            qi = (q[rs, :] * jnp.exp(b[rs, :] - b_ref)).astype(BF16)
            ki = (k * jnp.exp(jnp.minimum(b_ref - b, 0.0))).astype(BF16)
            parts.append(_dot_nt(qi, ki))
        p_off = jnp.concatenate(parts, axis=0)

        q4 = q.reshape(n_sub, SUB, GLA_HK)
        k4 = k.reshape(n_sub, SUB, GLA_HK)
        b4 = b.reshape(n_sub, SUB, GLA_HK)
        p_diag = jnp.zeros((CHUNK, CHUNK), F32)
        for jl in range(SUB):
            kj = k4[:, jl:jl + 1, :]
            bj = b4[:, jl:jl + 1, :]
            z = q4 * kj * jnp.exp(jnp.minimum(b4 - bj, 0.0))
            s = jnp.sum(z, axis=-1, keepdims=True).reshape(CHUNK, 1)
            p_diag = jnp.where(col_rel == jl, s, p_diag)
        p = jnp.where(off_mask, p_off, 0.0) + jnp.where(diag_mask, p_diag, 0.0)
        o = o + _dot(p.astype(BF16), v)

        o = o * lax.rsqrt(jnp.mean(o * o, axis=-1, keepdims=True) + EPS) * gn_ref[...]
        o_ref[rows, :] = (o * _silu(g_ref[rows, :])).astype(BF16)


def _gla(z, r, w2, gate_b, gn, *, batch, seq, t_blk=256):
    nt = seq // t_blk
    rowmap = lambda b, h, t: b * nt + t
    kq, kk, kv, kg = Z_Q // GLA_HK, Z_K // GLA_HK, Z_V // GLA_HV, Z_G // GLA_HV
    return pl.pallas_call(
        functools.partial(_gla_kernel, t_blk=t_blk),
        out_shape=jax.ShapeDtypeStruct((batch * seq, GLA_DV), BF16),
        grid=(batch, GLA_HEADS, nt),
        in_specs=[
            pl.BlockSpec((t_blk, GLA_HK), lambda b, h, t: (rowmap(b, h, t), kq + h)),
            pl.BlockSpec((t_blk, GLA_HK), lambda b, h, t: (rowmap(b, h, t), kk + h)),
            pl.BlockSpec((t_blk, GLA_HV), lambda b, h, t: (rowmap(b, h, t), kv + h)),
            pl.BlockSpec((t_blk, GLA_HV), lambda b, h, t: (rowmap(b, h, t), kg + h)),
            pl.BlockSpec((t_blk, R_PAD), lambda b, h, t: (rowmap(b, h, t), 0)),
            pl.BlockSpec((R_PAD, GLA_HK), lambda b, h, t: (0, h)),
            pl.BlockSpec((1, GLA_HK), lambda b, h, t: (0, h)),
            pl.BlockSpec((1, GLA_HV), lambda b, h, t: (0, h)),
        ],
        out_specs=pl.BlockSpec((t_blk, GLA_HV), lambda b, h, t: (rowmap(b, h, t), h)),
        scratch_shapes=[pltpu.VMEM((GLA_HV, GLA_HK), F32), pltpu.VMEM((t_blk, GLA_HK), F32)],
        compiler_params=_params(("parallel", "parallel", "arbitrary")),
        name="gla",
    )(z, z, z, z, r, w2, gate_b, gn)


def _conv_kernel(ua_ref, ug_ref, cw_ref, cb_ref, lg_ref, lb_ref, wp_ref, o_ref, cext_ref, cv_ref,
                 *, tc, cblk, rblk):
    t = pl.program_id(1)

    @pl.when(t == 0)
    def _():
        cext_ref[0:CONV_HALO, :] = jnp.zeros((CONV_HALO, D_MODEL), F32)

    @pl.when(t > 0)
    def _():
        cext_ref[0:CONV_HALO, :] = cext_ref[tc:tc + CONV_HALO, :]

    cext_ref[CONV_HALO:CONV_HALO + tc, :] = ua_ref[...] * jax.nn.sigmoid(ug_ref[...])

    row0 = CONV_HALO - (CONV_K - 1)

    def col_body(cb, carry):
        cols = pl.ds(pl.multiple_of(cb * cblk, cblk), cblk)
        w = cw_ref[:, cols]
        bias = cb_ref[:, cols]
        for rb in range(tc // rblk):
            acc = jnp.broadcast_to(bias, (rblk, cblk))
            for kk in range(CONV_K):
                acc = acc + w[kk:kk + 1, :] * cext_ref[pl.ds(rb * rblk + row0 + kk, rblk), cols]
            cv_ref[pl.ds(rb * rblk, rblk), cols] = acc
        return carry

    lax.fori_loop(0, D_MODEL // cblk, col_body, 0)

    c = cv_ref[...]
    mu = jnp.mean(c, axis=-1, keepdims=True)
    xc = c - mu
    y = xc * lax.rsqrt(jnp.mean(xc * xc, axis=-1, keepdims=True) + EPS) * lg_ref[...] + lb_ref[...]
    o_ref[...] = _dot(_silu(y).astype(BF16), wp_ref[...])


def _conv(z, conv_w, conv_b, ln_g, ln_b, w_proj, *, batch, seq, tc=256, cblk=256, rblk=64):
    nt = seq // tc
    rowmap = lambda b, t: b * nt + t
    return pl.pallas_call(
        functools.partial(_conv_kernel, tc=tc, cblk=cblk, rblk=rblk),
        out_shape=jax.ShapeDtypeStruct((batch * seq, D_MODEL), F32),
        grid=(batch, nt),
        in_specs=[
            pl.BlockSpec((tc, D_MODEL), lambda b, t: (rowmap(b, t), Z_UA // D_MODEL)),
            pl.BlockSpec((tc, D_MODEL), lambda b, t: (rowmap(b, t), Z_UG // D_MODEL)),
            pl.BlockSpec((CONV_K, D_MODEL), lambda b, t: (0, 0)),
            pl.BlockSpec((1, D_MODEL), lambda b, t: (0, 0)),
            pl.BlockSpec((1, D_MODEL), lambda b, t: (0, 0)),
            pl.BlockSpec((1, D_MODEL), lambda b, t: (0, 0)),
            pl.BlockSpec((D_MODEL, D_MODEL), lambda b, t: (0, 0)),
        ],
        out_specs=pl.BlockSpec((tc, D_MODEL), lambda b, t: (rowmap(b, t), 0)),
        scratch_shapes=[pltpu.VMEM((tc + CONV_HALO, D_MODEL), F32), pltpu.VMEM((tc, D_MODEL), F32)],
        compiler_params=_params(("parallel", "arbitrary")),
        name="conv",
    )(z, z, conv_w, conv_b, ln_g, ln_b, w_proj)


def _merge_kernel(x_ref, g0_ref, g1_ref, ya_ref, yb_ref, b0_ref, b1_ref, w_ref, o_ref):
    m = (jax.nn.sigmoid(g0_ref[...] + b0_ref[...]) * ya_ref[...]
         + jax.nn.sigmoid(g1_ref[...] + b1_ref[...]) * yb_ref[...])
    o_ref[...] = x_ref[...] + _dot(m.astype(BF16), w_ref[...])


def _merge(x, z, ya, yb, gate_b, w_out, *, tm=256):
    n = x.shape[0]
    row = lambda i: (i, 0)
    return pl.pallas_call(
        _merge_kernel,
        out_shape=jax.ShapeDtypeStruct((n, D_MODEL), F32),
        grid=(n // tm,),
        in_specs=[
            pl.BlockSpec((tm, D_MODEL), row),
            pl.BlockSpec((tm, D_MODEL), lambda i: (i, Z_GT0 // D_MODEL)),
            pl.BlockSpec((tm, D_MODEL), lambda i: (i, Z_GT1 // D_MODEL)),
            pl.BlockSpec((tm, D_MODEL), row),
            pl.BlockSpec((tm, D_MODEL), row),
            pl.BlockSpec((1, D_MODEL), lambda i: (0, 0)),
            pl.BlockSpec((1, D_MODEL), lambda i: (0, 1)),
            pl.BlockSpec((D_MODEL, D_MODEL), lambda i: (0, 0)),
        ],
        out_specs=pl.BlockSpec((tm, D_MODEL), row),
        compiler_params=_params(("parallel",)),
        name="merge",
    )(x, z, z, ya, yb, gate_b, gate_b, w_out)


def _xattn_kernel(x_ref, g_ref, wq_ref, k_ref, v_ref, wo_ref, o_ref):
    x = x_ref[...]
    h = _rms(x, g_ref[...]).astype(BF16)
    q = _dot(h, wq_ref[...]).astype(BF16)
    k = k_ref[...].astype(BF16)
    v = v_ref[...].astype(BF16)
    outs = []
    for hd in range(XA_HEADS):
        cs = slice(hd * XA_HEAD_DIM, (hd + 1) * XA_HEAD_DIM)
        s = _dot_nt(q[:, cs], k[:, cs]) * (XA_HEAD_DIM ** -0.5)
        e = jnp.exp(s - jnp.max(s, axis=-1, keepdims=True))
        p = e / jnp.sum(e, axis=-1, keepdims=True)
        outs.append(_dot(p.astype(BF16), v[:, cs]))
    o = jnp.concatenate(outs, axis=-1).astype(BF16)
    o_ref[...] = x + _dot(o, wo_ref[...])


def _xattn(x, kv, norm_g, w_q, w_out, *, batch, seq, n_mem, tm=512):
    nt = seq // tm
    return pl.pallas_call(
        _xattn_kernel,
        out_shape=jax.ShapeDtypeStruct((batch * seq, D_MODEL), F32),
        grid=(batch, nt),
        in_specs=[
            pl.BlockSpec((tm, D_MODEL), lambda b, t: (b * nt + t, 0)),
            pl.BlockSpec((1, D_MODEL), lambda b, t: (0, 0)),
            pl.BlockSpec((D_MODEL, XA_WIDTH), lambda b, t: (0, 0)),
            pl.BlockSpec((n_mem, XA_WIDTH), lambda b, t: (b, 0)),
            pl.BlockSpec((n_mem, XA_WIDTH), lambda b, t: (b, 1)),
            pl.BlockSpec((XA_WIDTH, D_MODEL), lambda b, t: (0, 0)),
        ],
        out_specs=pl.BlockSpec((tm, D_MODEL), lambda b, t: (b * nt + t, 0)),
        compiler_params=_params(("parallel", "parallel")),
        name="xattn",
    )(x, norm_g, w_q, kv, kv, w_out)


def _row(v):
    return v.reshape(1, -1)


def kernel(x, mem, ffn1_norm, ffn1_w_in, ffn1_w_out, mix_norm, mix_w_in, gla_gate_w2, gla_gate_b, gla_out_norm, gla_proj, conv_w, conv_b, conv_ln_g, conv_ln_b, conv_proj, branch_gate_b, mix_w_out, xa_norm, xa_mem_norm, xa_w_q, xa_w_kv, xa_w_out, ffn2_norm, ffn2_w_in, ffn2_w_out, final_norm):
    batch, seq, _ = x.shape
    n_mem = mem.shape[1]
    depth = ffn1_norm.shape[0]
    xs = x.reshape(batch * seq, D_MODEL)
    mems = mem.reshape(batch * n_mem, D_MODEL)
    r0 = 2 * GLA_DK + GLA_DV
    r1 = r0 + GLA_RANK
    fin = _row(final_norm)

    for l in range(depth):
        w_mix = mix_w_in[l]
        w_z = jnp.concatenate([w_mix[:, :r0], w_mix[:, r1:]], axis=1).astype(BF16)
        w_r = jnp.pad(w_mix[:, r0:r1], ((0, 0), (0, R_PAD - GLA_RANK))).astype(BF16)
        w2 = jnp.pad(gla_gate_w2[l], ((0, R_PAD - GLA_RANK), (0, 0))).astype(BF16)

        xs = _ffn(xs, _row(ffn1_norm[l]), ffn1_w_in[l].astype(BF16), ffn1_w_out[l].astype(BF16),
                  fin, final_norm=False)

        z, r = _norm_proj(xs, _row(mix_norm[l]), w_z, w_r, tm=1024, tn=512)
        og = _gla(z, r, w2, _row(gla_gate_b[l]), _row(gla_out_norm[l]), batch=batch, seq=seq)
        ya = _matmul(og, gla_proj[l].astype(BF16), tm=512)
        yb = _conv(z, conv_w[l], _row(conv_b[l]), _row(conv_ln_g[l]), _row(conv_ln_b[l]),
                   conv_proj[l].astype(BF16), batch=batch, seq=seq)
        xs = _merge(xs, z, ya, yb, _row(branch_gate_b[l]), mix_w_out[l].astype(BF16))

        kv = _norm_matmul(mems, _row(xa_mem_norm[l]), xa_w_kv[l].astype(BF16), tm=batch * n_mem)
        xs = _xattn(xs, kv, _row(xa_norm[l]), xa_w_q[l].astype(BF16), xa_w_out[l].astype(BF16),
                    batch=batch, seq=seq, n_mem=n_mem)

        xs = _ffn(xs, _row(ffn2_norm[l]), ffn2_w_in[l].astype(BF16), ffn2_w_out[l].astype(BF16),
                  fin, final_norm=(l == depth - 1))

    return xs.reshape(batch, seq, D_MODEL)
```

```python
import functools
import math

import jax
import jax.numpy as jnp
from jax import lax
from jax.experimental import pallas as pl
from jax.experimental.pallas import tpu as pltpu

F32 = jnp.float32
BF16 = jnp.bfloat16

D_MODEL = 2048
CHUNK = 64
SUB = 16
SUBLANES = 8
GLA_HEADS = 4
GLA_DK = D_MODEL // 2
GLA_DV = D_MODEL
GLA_HK = GLA_DK // GLA_HEADS
GLA_HV = GLA_DV // GLA_HEADS
GLA_RANK = 16
GLA_TAU = 16.0
CONV_K = 31
CONV_HALO = 32
XA_HEADS = 4
XA_HEAD_DIM = 128
XA_WIDTH = XA_HEADS * XA_HEAD_DIM
D_FF = 5632
FFN_RES = 0.5
EPS = 1e-6
LANE = 128
R_PAD = LANE
LOG2E = math.log2(math.e)

VMEM_LIMIT = 56 * 1024 * 1024


def _params(sem):
    return pltpu.CompilerParams(dimension_semantics=sem, vmem_limit_bytes=VMEM_LIMIT)


def _rms(x, g):
    return x * lax.rsqrt(jnp.mean(x * x, axis=-1, keepdims=True) + EPS) * g


def _silu(x):
    return x * jax.nn.sigmoid(x)


def _dot(a, b):
    return jnp.dot(a, b, preferred_element_type=F32)


def _dot_nt(a, b):
    return lax.dot_general(a, b, (((1,), (1,)), ((), ())), preferred_element_type=F32)


def _dot_tn(a, b):
    return lax.dot_general(a, b, (((0,), (0,)), ((), ())), preferred_element_type=F32)


def _resident(shape, index_map):
    return pl.BlockSpec(shape, index_map, pipeline_mode=pl.Buffered(1))


def _ffn_kernel(x_ref, g_ref, wa_ref, wb_ref, w2_ref, fg_ref, o_ref, h_ref, *, final_norm):
    j = pl.program_id(1)

    @pl.when(j == 0)
    def _():
        x = x_ref[...]
        h_ref[...] = _rms(x, g_ref[...]).astype(BF16)
        o_ref[...] = x

    h = h_ref[...]
    a = _dot(h, wa_ref[...])
    b = _dot(h, wb_ref[...])
    g = (_silu(a) * b * FFN_RES).astype(BF16)
    o_ref[...] += _dot(g, w2_ref[...])

    if final_norm:
        @pl.when(j == pl.num_programs(1) - 1)
        def _():
            o_ref[...] = _rms(o_ref[...], fg_ref[...])


def _ffn(x, norm_g, w_in, w_out, final_g, *, final_norm, tm=1024, tf=512):
    n = x.shape[0]
    nf = D_FF // tf
    return pl.pallas_call(
        functools.partial(_ffn_kernel, final_norm=final_norm),
        out_shape=jax.ShapeDtypeStruct((n, D_MODEL), F32),
        grid=(n // tm, nf),
        in_specs=[
            pl.BlockSpec((tm, D_MODEL), lambda i, j: (i, 0)),
            pl.BlockSpec((1, D_MODEL), lambda i, j: (0, 0)),
            pl.BlockSpec((D_MODEL, tf), lambda i, j: (0, j)),
            pl.BlockSpec((D_MODEL, tf), lambda i, j: (0, j + nf)),
            pl.BlockSpec((tf, D_MODEL), lambda i, j: (j, 0)),
            pl.BlockSpec((1, D_MODEL), lambda i, j: (0, 0)),
        ],
        out_specs=pl.BlockSpec((tm, D_MODEL), lambda i, j: (i, 0)),
        scratch_shapes=[pltpu.VMEM((tm, D_MODEL), BF16)],
        compiler_params=_params(("parallel", "arbitrary")),
        name="ffn",
    )(x, norm_g, w_in, w_in, w_out, final_g)


def _proj_kernel(x_ref, g_ref, wa_ref, wb_ref, wr_ref, gb_ref,
                 qk_ref, v_ref, sg_ref, c_ref, gt_ref, r_ref, h_ref, *, bounds):
    j = pl.program_id(1)
    e_qk, e_v, e_g, e_c = bounds

    @pl.when(j == 0)
    def _():
        h = _rms(x_ref[...], g_ref[...]).astype(BF16)
        h_ref[...] = h
        r_ref[...] = _dot(h, wr_ref[...])

    @pl.when(j < e_qk)
    def _():
        qk_ref[...] = _dot(h_ref[...], wa_ref[...])

    @pl.when((j >= e_qk) & (j < e_v))
    def _():
        v_ref[...] = _dot(h_ref[...], wa_ref[...]).astype(BF16)

    @pl.when((j >= e_v) & (j < e_g))
    def _():
        sg_ref[...] = _silu(_dot(h_ref[...], wa_ref[...])).astype(BF16)

    @pl.when((j >= e_g) & (j < e_c))
    def _():
        h = h_ref[...]
        c_ref[...] = _dot(h, wa_ref[...]) * jax.nn.sigmoid(_dot(h, wb_ref[...]))

    @pl.when(j >= e_c)
    def _():
        gt_ref[...] = jax.nn.sigmoid(_dot(h_ref[...], wa_ref[...]) + gb_ref[...]).astype(BF16)


def _proj(x, norm_g, w_z, w_r, gate_b, *, tm=1024, tn=512):
    n = x.shape[0]
    n_qk, n_v, n_g, n_c, n_gt = (2 * GLA_DK // tn, GLA_DV // tn, GLA_DV // tn, D_MODEL // tn,
                                 2 * D_MODEL // tn)
    e_qk = n_qk
    e_v = e_qk + n_v
    e_g = e_v + n_g
    e_c = e_g + n_c
    steps = e_c + n_gt

    def park(j, lo, cnt):
        return jnp.clip(j - lo, 0, cnt - 1)

    return pl.pallas_call(
        functools.partial(_proj_kernel, bounds=(e_qk, e_v, e_g, e_c)),
        out_shape=(jax.ShapeDtypeStruct((n, 2 * GLA_DK), F32),
                   jax.ShapeDtypeStruct((n, GLA_DV), BF16),
                   jax.ShapeDtypeStruct((n, GLA_DV), BF16),
                   jax.ShapeDtypeStruct((n, D_MODEL), F32),
                   jax.ShapeDtypeStruct((n, 2 * D_MODEL), BF16),
                   jax.ShapeDtypeStruct((n, R_PAD), F32)),
        grid=(n // tm, steps),
        in_specs=[
            pl.BlockSpec((tm, D_MODEL), lambda i, j: (i, 0)),
            pl.BlockSpec((1, D_MODEL), lambda i, j: (0, 0)),
            pl.BlockSpec((D_MODEL, tn), lambda i, j: (0, jnp.where(j < e_c, j, j + n_c))),
            pl.BlockSpec((D_MODEL, tn), lambda i, j: (0, e_c + park(j, e_g, n_c))),
            pl.BlockSpec((D_MODEL, R_PAD), lambda i, j: (0, 0)),
            pl.BlockSpec((1, tn), lambda i, j: (0, park(j, e_c, n_gt))),
        ],
        out_specs=(pl.BlockSpec((tm, tn), lambda i, j: (i, park(j, 0, n_qk))),
                   pl.BlockSpec((tm, tn), lambda i, j: (i, park(j, e_qk, n_v))),
                   pl.BlockSpec((tm, tn), lambda i, j: (i, park(j, e_v, n_g))),
                   pl.BlockSpec((tm, tn), lambda i, j: (i, park(j, e_g, n_c))),
                   pl.BlockSpec((tm, tn), lambda i, j: (i, park(j, e_c, n_gt))),
                   pl.BlockSpec((tm, R_PAD), lambda i, j: (i, 0))),
        scratch_shapes=[pltpu.VMEM((tm, D_MODEL), BF16)],
        compiler_params=_params(("parallel", "arbitrary")),
        name="proj",
    )(x, norm_g, w_z, w_z, w_r, gate_b)


def _norm_matmul_kernel(x_ref, g_ref, w_ref, o_ref):
    h = _rms(x_ref[...], g_ref[...]).astype(BF16)
    o_ref[...] = _dot(h, w_ref[...])


def _norm_matmul(x, norm_g, w, *, tm):
    n = x.shape[0]
    nout = w.shape[1]
    return pl.pallas_call(
        _norm_matmul_kernel,
        out_shape=jax.ShapeDtypeStruct((n, nout), F32),
        grid=(n // tm,),
        in_specs=[
            pl.BlockSpec((tm, D_MODEL), lambda i: (i, 0)),
            pl.BlockSpec((1, D_MODEL), lambda i: (0, 0)),
            pl.BlockSpec((D_MODEL, nout), lambda i: (0, 0)),
        ],
        out_specs=pl.BlockSpec((tm, nout), lambda i: (i, 0)),
        compiler_params=_params(("parallel",)),
        name="norm_matmul",
    )(x, norm_g, w)


def _split3(x):
    hi = x.astype(BF16)
    r1 = x - hi.astype(F32)
    mid = r1.astype(BF16)
    lo = (r1 - mid.astype(F32)).astype(BF16)
    return hi, mid, lo


def _gla_kernel(q_ref, k_ref, v_ref, sg_ref, r_ref, w2_ref, gb_ref, gn_ref, o_ref, st_ref, bc_ref,
                *, t_blk):
    @pl.when(pl.program_id(2) == 0)
    def _():
        st_ref[...] = jnp.zeros_like(st_ref)

    n_sub = CHUNK // SUB
    row = lax.broadcasted_iota(jnp.int32, (CHUNK, CHUNK), 0)
    col = lax.broadcasted_iota(jnp.int32, (CHUNK, CHUNK), 1)
    tri = (row >= col).astype(BF16)
    causal = col <= row

    pre = _dot(r_ref[...].astype(BF16), w2_ref[...]) + gb_ref[...]
    la = (jnp.minimum(pre, 0.0) - jnp.log1p(jnp.exp(-jnp.abs(pre)))) * (1.0 / GLA_TAU)
    hi, mid, lo = _split3(la)
    for c in range(t_blk // CHUNK):
        rs = slice(c * CHUNK, (c + 1) * CHUNK)
        bc_ref[rs, :] = _dot(tri, hi[rs, :]) + _dot(tri, mid[rs, :]) + _dot(tri, lo[rs, :])

    half_shape = (n_sub, SUBLANES, CHUNK)
    col_rel = (lax.broadcasted_iota(jnp.int32, half_shape, 2)
               - SUB * lax.broadcasted_iota(jnp.int32, half_shape, 0))

    for c in range(t_blk // CHUNK):
        rows = pl.ds(c * CHUNK, CHUNK)
        q = q_ref[rows, :] * (GLA_HK ** -0.5)
        k = k_ref[rows, :]
        v = v_ref[rows, :]
        b = bc_ref[rows, :]
        b_last = b[CHUNK - 1:CHUNK, :]

        st = st_ref[...]
        o = _dot_nt((q * jnp.exp(b)).astype(BF16), st.astype(BF16))
        kd = (k * jnp.exp(b_last - b)).astype(BF16)
        st_ref[...] = st * jnp.exp(b_last) + _dot_tn(v, kd)

        parts = [jnp.zeros((SUB, CHUNK), F32)]
        for i in range(1, n_sub):
            b_start = b[i * SUB - 1:i * SUB, :]
            rs = slice(i * SUB, (i + 1) * SUB)
            qi = (q[rs, :] * jnp.exp(b[rs, :] - b_start)).astype(BF16)
            ki = (k[:i * SUB, :] * jnp.exp(b_start - b[:i * SUB, :])).astype(BF16)
            ki = jnp.concatenate([ki, jnp.zeros((CHUNK - i * SUB, GLA_HK), BF16)], axis=0)
            parts.append(_dot_nt(qi, ki))
        p_off = jnp.concatenate(parts, axis=0)

        q4 = q.reshape(n_sub, SUB, GLA_HK)
        k4 = k.reshape(n_sub, SUB, GLA_HK)
        b4 = (b * LOG2E).reshape(n_sub, SUB, GLA_HK)
        q_hi, b_hi = q4[:, SUBLANES:, :], b4[:, SUBLANES:, :]
        p_lo = jnp.zeros(half_shape, F32)
        p_hi = jnp.zeros(half_shape, F32)
        for jl in range(SUB):
            kj = k4[:, jl:jl + 1, :]
            bj = b4[:, jl:jl + 1, :]
            if jl < SUBLANES:
                s = jnp.sum(q4 * kj * jnp.exp2(b4 - bj), axis=-1, keepdims=True)
                p_lo = jnp.where(col_rel == jl, s[:, :SUBLANES, :], p_lo)
                p_hi = jnp.where(col_rel == jl, s[:, SUBLANES:, :], p_hi)
            else:
                s = jnp.sum(q_hi * kj * jnp.exp2(b_hi - bj), axis=-1, keepdims=True)
                p_hi = jnp.where(col_rel == jl, s, p_hi)
        p_diag = jnp.concatenate([p_lo, p_hi], axis=1).reshape(CHUNK, CHUNK)
        p = p_off + jnp.where(causal, p_diag, 0.0)
        o = o + _dot(p.astype(BF16), v)

        o = o * lax.rsqrt(jnp.mean(o * o, axis=-1, keepdims=True) + EPS) * gn_ref[...]
        o_ref[rows, :] = (o * sg_ref[rows, :].astype(F32)).astype(BF16)


def _gla(qk, v, sg, r, w2, gate_b, gn, *, batch, seq, t_blk=512):
    nt = seq // t_blk
    rowmap = lambda b, h, t: b * nt + t
    return pl.pallas_call(
        functools.partial(_gla_kernel, t_blk=t_blk),
        out_shape=jax.ShapeDtypeStruct((batch * seq, GLA_DV), BF16),
        grid=(batch, GLA_HEADS, nt),
        in_specs=[
            pl.BlockSpec((t_blk, GLA_HK), lambda b, h, t: (rowmap(b, h, t), h)),
            pl.BlockSpec((t_blk, GLA_HK), lambda b, h, t: (rowmap(b, h, t), GLA_HEADS + h)),
            pl.BlockSpec((t_blk, GLA_HV), lambda b, h, t: (rowmap(b, h, t), h)),
            pl.BlockSpec((t_blk, GLA_HV), lambda b, h, t: (rowmap(b, h, t), h)),
            pl.BlockSpec((t_blk, R_PAD), lambda b, h, t: (rowmap(b, h, t), 0)),
            pl.BlockSpec((R_PAD, GLA_HK), lambda b, h, t: (0, h)),
            pl.BlockSpec((1, GLA_HK), lambda b, h, t: (0, h)),
            pl.BlockSpec((1, GLA_HV), lambda b, h, t: (0, h)),
        ],
        out_specs=pl.BlockSpec((t_blk, GLA_HV), lambda b, h, t: (rowmap(b, h, t), h)),
        scratch_shapes=[pltpu.VMEM((GLA_HV, GLA_HK), F32), pltpu.VMEM((t_blk, GLA_HK), F32)],
        compiler_params=_params(("parallel", "parallel", "arbitrary")),
        name="gla",
    )(qk, qk, v, sg, r, w2, gate_b, gn)


def _conv_kernel(c_ref, cw_ref, cb_ref, lg_ref, lb_ref, o_ref, cext_ref, xs_ref, cv_ref,
                 *, tc, cblk, rblk):
    t = pl.program_id(1)

    @pl.when(t == 0)
    def _():
        cext_ref[0:CONV_HALO, :] = jnp.zeros((CONV_HALO, D_MODEL), F32)

    @pl.when(t > 0)
    def _():
        cext_ref[0:CONV_HALO, :] = cext_ref[tc:tc + CONV_HALO, :]

    cext_ref[CONV_HALO:CONV_HALO + tc, :] = c_ref[...]

    row0 = CONV_HALO - (CONV_K - 1)
    n_shift = tc + CONV_HALO - SUBLANES

    def col_body(cb, carry):
        cols = pl.ds(pl.multiple_of(cb * cblk, cblk), cblk)
        for s in range(1, SUBLANES):
            xs_ref[s, 0:n_shift, :] = cext_ref[pl.ds(s, n_shift), cols]
        w = cw_ref[:, cols]
        bias = cb_ref[:, cols]
        for rb in range(tc // rblk):
            acc = jnp.broadcast_to(bias, (rblk, cblk))
            for kk in range(CONV_K):
                s = (row0 + kk) % SUBLANES
                base = rb * rblk + (row0 + kk) - s
                if s == 0:
                    win = cext_ref[pl.ds(base, rblk), cols]
                else:
                    win = xs_ref[s, pl.ds(base, rblk), :]
                acc = acc + w[kk:kk + 1, :] * win
            cv_ref[pl.ds(rb * rblk, rblk), cols] = acc
        return carry

    lax.fori_loop(0, D_MODEL // cblk, col_body, 0)

    c = cv_ref[...]
    mu = jnp.mean(c, axis=-1, keepdims=True)
    xc = c - mu
    y = xc * lax.rsqrt(jnp.mean(xc * xc, axis=-1, keepdims=True) + EPS) * lg_ref[...] + lb_ref[...]
    o_ref[...] = _silu(y).astype(BF16)


def _conv(c, conv_w, conv_b, ln_g, ln_b, *, batch, seq, tc=256, cblk=256, rblk=64):
    nt = seq // tc
    return pl.pallas_call(
        functools.partial(_conv_kernel, tc=tc, cblk=cblk, rblk=rblk),
        out_shape=jax.ShapeDtypeStruct((batch * seq, D_MODEL), BF16),
        grid=(batch, nt),
        in_specs=[
            pl.BlockSpec((tc, D_MODEL), lambda b, t: (b * nt + t, 0)),
            pl.BlockSpec((CONV_K, D_MODEL), lambda b, t: (0, 0)),
            pl.BlockSpec((1, D_MODEL), lambda b, t: (0, 0)),
            pl.BlockSpec((1, D_MODEL), lambda b, t: (0, 0)),
            pl.BlockSpec((1, D_MODEL), lambda b, t: (0, 0)),
        ],
        out_specs=pl.BlockSpec((tc, D_MODEL), lambda b, t: (b * nt + t, 0)),
        scratch_shapes=[pltpu.VMEM((tc + CONV_HALO, D_MODEL), F32),
                        pltpu.VMEM((SUBLANES, tc + CONV_HALO, cblk), F32),
                        pltpu.VMEM((tc, D_MODEL), F32)],
        compiler_params=_params(("parallel", "arbitrary")),
        name="conv",
    )(c, conv_w, conv_b, ln_g, ln_b)


def _merge_kernel(x_ref, og_ref, cn_ref, g0_ref, g1_ref, wg_ref, wc_ref, wo_ref, o_ref):
    ya = _dot(og_ref[...], wg_ref[...])
    yb = _dot(cn_ref[...], wc_ref[...])
    m = g0_ref[...].astype(F32) * ya + g1_ref[...].astype(F32) * yb
    o_ref[...] = x_ref[...] + _dot(m.astype(BF16), wo_ref[...])


def _merge(x, og, cn, gates, w_gla, w_conv, w_out, *, tm=256):
    n = x.shape[0]
    row = lambda i: (i, 0)
    wspec = _resident((D_MODEL, D_MODEL), lambda i: (0, 0))
    return pl.pallas_call(
        _merge_kernel,
        out_shape=jax.ShapeDtypeStruct((n, D_MODEL), F32),
        grid=(n // tm,),
        in_specs=[
            pl.BlockSpec((tm, D_MODEL), row),
            pl.BlockSpec((tm, D_MODEL), row),
            pl.BlockSpec((tm, D_MODEL), row),
            pl.BlockSpec((tm, D_MODEL), lambda i: (i, 0)),
            pl.BlockSpec((tm, D_MODEL), lambda i: (i, 1)),
            wspec, wspec, wspec,
        ],
        out_specs=pl.BlockSpec((tm, D_MODEL), row),
        compiler_params=_params(("parallel",)),
        name="merge",
    )(x, og, cn, gates, gates, w_gla, w_conv, w_out)


def _xattn_kernel(x_ref, g_ref, wq_ref, k_ref, v_ref, wo_ref, o_ref):
    x = x_ref[...]
    h = _rms(x, g_ref[...]).astype(BF16)
    q = _dot(h, wq_ref[...]).astype(BF16)
    k = k_ref[...].astype(BF16)
    v = v_ref[...].astype(BF16)
    outs = []
    for hd in range(XA_HEADS):
        cs = slice(hd * XA_HEAD_DIM, (hd + 1) * XA_HEAD_DIM)
        s = _dot_nt(q[:, cs], k[:, cs]) * (XA_HEAD_DIM ** -0.5)
        e = jnp.exp(s - jnp.max(s, axis=-1, keepdims=True))
        p = e / jnp.sum(e, axis=-1, keepdims=True)
        outs.append(_dot(p.astype(BF16), v[:, cs]))
    o = jnp.concatenate(outs, axis=-1).astype(BF16)
    o_ref[...] = x + _dot(o, wo_ref[...])


def _xattn(x, kv, norm_g, w_q, w_out, *, batch, seq, n_mem, tm=512):
    nt = seq // tm
    return pl.pallas_call(
        _xattn_kernel,
        out_shape=jax.ShapeDtypeStruct((batch * seq, D_MODEL), F32),
        grid=(batch, nt),
        in_specs=[
            pl.BlockSpec((tm, D_MODEL), lambda b, t: (b * nt + t, 0)),
            pl.BlockSpec((1, D_MODEL), lambda b, t: (0, 0)),
            pl.BlockSpec((D_MODEL, XA_WIDTH), lambda b, t: (0, 0)),
            pl.BlockSpec((n_mem, XA_WIDTH), lambda b, t: (b, 0)),
            pl.BlockSpec((n_mem, XA_WIDTH), lambda b, t: (b, 1)),
            pl.BlockSpec((XA_WIDTH, D_MODEL), lambda b, t: (0, 0)),
        ],
        out_specs=pl.BlockSpec((tm, D_MODEL), lambda b, t: (b * nt + t, 0)),
        compiler_params=_params(("parallel", "parallel")),
        name="xattn",
    )(x, norm_g, w_q, kv, kv, w_out)


def _row(v):
    return v.reshape(1, -1)


def kernel(x, mem, ffn1_norm, ffn1_w_in, ffn1_w_out, mix_norm, mix_w_in, gla_gate_w2, gla_gate_b, gla_out_norm, gla_proj, conv_w, conv_b, conv_ln_g, conv_ln_b, conv_proj, branch_gate_b, mix_w_out, xa_norm, xa_mem_norm, xa_w_q, xa_w_kv, xa_w_out, ffn2_norm, ffn2_w_in, ffn2_w_out, final_norm):
    batch, seq, _ = x.shape
    n_mem = mem.shape[1]
    depth = ffn1_norm.shape[0]
    xs = x.reshape(batch * seq, D_MODEL)
    mems = mem.reshape(batch * n_mem, D_MODEL)
    r0 = 2 * GLA_DK + GLA_DV
    r1 = r0 + GLA_RANK
    fin = _row(final_norm)

    for l in range(depth):
        w_mix = mix_w_in[l]
        w_z = jnp.concatenate([w_mix[:, :r0], w_mix[:, r1:]], axis=1).astype(BF16)
        w_r = jnp.pad(w_mix[:, r0:r1], ((0, 0), (0, R_PAD - GLA_RANK))).astype(BF16)
        w2 = jnp.pad(gla_gate_w2[l], ((0, R_PAD - GLA_RANK), (0, 0))).astype(BF16)

        xs = _ffn(xs, _row(ffn1_norm[l]), ffn1_w_in[l].astype(BF16), ffn1_w_out[l].astype(BF16),
                  fin, final_norm=False)

        qk, v, sg, c, gates, r = _proj(xs, _row(mix_norm[l]), w_z, w_r, _row(branch_gate_b[l]))
        og = _gla(qk, v, sg, r, w2, _row(gla_gate_b[l]), _row(gla_out_norm[l]),
                  batch=batch, seq=seq)
        cn = _conv(c, conv_w[l], _row(conv_b[l]), _row(conv_ln_g[l]), _row(conv_ln_b[l]),
                   batch=batch, seq=seq)
        xs = _merge(xs, og, cn, gates, gla_proj[l].astype(BF16), conv_proj[l].astype(BF16),
                    mix_w_out[l].astype(BF16))

        kv = _norm_matmul(mems, _row(xa_mem_norm[l]), xa_w_kv[l].astype(BF16), tm=batch * n_mem)
        xs = _xattn(xs, kv, _row(xa_norm[l]), xa_w_q[l].astype(BF16), xa_w_out[l].astype(BF16),
                    batch=batch, seq=seq, n_mem=n_mem)

        xs = _ffn(xs, _row(ffn2_norm[l]), ffn2_w_in[l].astype(BF16), ffn2_w_out[l].astype(BF16),
                  fin, final_norm=(l == depth - 1))

    return xs.reshape(batch, seq, D_MODEL)
```

```python
import functools
import math

import jax
import jax.numpy as jnp
from jax import lax
from jax.experimental import pallas as pl
from jax.experimental.pallas import tpu as pltpu

F32 = jnp.float32
BF16 = jnp.bfloat16

D_MODEL = 2048
CHUNK = 64
SUB = 16
SUBLANES = 8
GLA_HEADS = 4
GLA_DK = D_MODEL // 2
GLA_DV = D_MODEL
GLA_HK = GLA_DK // GLA_HEADS
GLA_HV = GLA_DV // GLA_HEADS
GLA_RANK = 16
GLA_TAU = 16.0
CONV_K = 31
CONV_HALO = 32
XA_HEADS = 4
XA_HEAD_DIM = 128
XA_WIDTH = XA_HEADS * XA_HEAD_DIM
D_FF = 5632
FFN_RES = 0.5
EPS = 1e-6
LANE = 128
R_PAD = LANE
LOG2E = math.log2(math.e)

VMEM_LIMIT = 56 * 1024 * 1024


def _params(sem):
    return pltpu.CompilerParams(dimension_semantics=sem, vmem_limit_bytes=VMEM_LIMIT)


def _rms(x, g):
    return x * lax.rsqrt(jnp.mean(x * x, axis=-1, keepdims=True) + EPS) * g


def _silu(x):
    return x * jax.nn.sigmoid(x)


def _dot(a, b):
    return jnp.dot(a, b, preferred_element_type=F32)


def _dot_nt(a, b):
    return lax.dot_general(a, b, (((1,), (1,)), ((), ())), preferred_element_type=F32)


def _dot_tn(a, b):
    return lax.dot_general(a, b, (((0,), (0,)), ((), ())), preferred_element_type=F32)


def _resident(shape, index_map):
    return pl.BlockSpec(shape, index_map, pipeline_mode=pl.Buffered(1))


def _cast_kernel(w_ref, o_ref):
    o_ref[...] = w_ref[...].astype(BF16)


def _cast_layer(w, l, *, block_elems=1 << 20):
    _, r, c = w.shape
    tr = min(r, 1 << int(math.log2(block_elems // c)))
    assert r % tr == 0
    return pl.pallas_call(
        _cast_kernel,
        out_shape=jax.ShapeDtypeStruct((r, c), BF16),
        grid=(r // tr,),
        in_specs=[pl.BlockSpec((None, tr, c), lambda i: (l, i, 0))],
        out_specs=pl.BlockSpec((tr, c), lambda i: (i, 0)),
        compiler_params=_params(("parallel",)),
        name="cast",
    )(w)


def _ffn_kernel(x_ref, g_ref, wa_ref, wb_ref, w2_ref, fg_ref, o_ref, h_ref, *, final_norm):
    j = pl.program_id(1)

    @pl.when(j == 0)
    def _():
        x = x_ref[...]
        h_ref[...] = _rms(x, g_ref[...]).astype(BF16)
        o_ref[...] = x

    h = h_ref[...]
    a = _dot(h, wa_ref[...])
    b = _dot(h, wb_ref[...])
    g = (_silu(a) * b * FFN_RES).astype(BF16)
    o_ref[...] += _dot(g, w2_ref[...])

    if final_norm:
        @pl.when(j == pl.num_programs(1) - 1)
        def _():
            o_ref[...] = _rms(o_ref[...], fg_ref[...])


def _ffn(x, norm_g, w_in, w_out, final_g, *, final_norm, tm=1024, tf=512):
    n = x.shape[0]
    nf = D_FF // tf
    return pl.pallas_call(
        functools.partial(_ffn_kernel, final_norm=final_norm),
        out_shape=jax.ShapeDtypeStruct((n, D_MODEL), F32),
        grid=(n // tm, nf),
        in_specs=[
            pl.BlockSpec((tm, D_MODEL), lambda i, j: (i, 0)),
            pl.BlockSpec((1, D_MODEL), lambda i, j: (0, 0)),
            pl.BlockSpec((D_MODEL, tf), lambda i, j: (0, j)),
            pl.BlockSpec((D_MODEL, tf), lambda i, j: (0, j + nf)),
            pl.BlockSpec((tf, D_MODEL), lambda i, j: (j, 0)),
            pl.BlockSpec((1, D_MODEL), lambda i, j: (0, 0)),
        ],
        out_specs=pl.BlockSpec((tm, D_MODEL), lambda i, j: (i, 0)),
        scratch_shapes=[pltpu.VMEM((tm, D_MODEL), BF16)],
        compiler_params=_params(("parallel", "arbitrary")),
        name="ffn",
    )(x, norm_g, w_in, w_in, w_out, final_g)


def _proj_kernel(x_ref, g_ref, wa_ref, wb_ref, wr_ref, gb_ref,
                 qk_ref, v_ref, sg_ref, c_ref, gt_ref, r_ref, h_ref, *, bounds):
    j = pl.program_id(1)
    e_qk, e_v, e_g, e_c = bounds

    @pl.when(j == 0)
    def _():
        h = _rms(x_ref[...], g_ref[...]).astype(BF16)
        h_ref[...] = h
        r_ref[...] = _dot(h, wr_ref[...])

    @pl.when(j < e_qk)
    def _():
        qk_ref[...] = _dot(h_ref[...], wa_ref[...])

    @pl.when((j >= e_qk) & (j < e_v))
    def _():
        v_ref[...] = _dot(h_ref[...], wa_ref[...]).astype(BF16)

    half = h_ref.shape[0] // 2
    halves = (pl.ds(0, half), pl.ds(half, half))

    @pl.when((j >= e_v) & (j < e_g))
    def _():
        for rs in halves:
            sg_ref[rs, :] = _silu(_dot(h_ref[rs, :], wa_ref[...])).astype(BF16)

    @pl.when((j >= e_g) & (j < e_c))
    def _():
        for rs in halves:
            h = h_ref[rs, :]
            c_ref[rs, :] = _dot(h, wa_ref[...]) * jax.nn.sigmoid(_dot(h, wb_ref[...]))

    @pl.when(j >= e_c)
    def _():
        for rs in halves:
            gt_ref[rs, :] = jax.nn.sigmoid(_dot(h_ref[rs, :], wa_ref[...])
                                           + gb_ref[...]).astype(BF16)


def _proj(x, norm_g, w_z, w_r, gate_b, *, tm=1024, tn=512):
    n = x.shape[0]
    n_qk, n_v, n_g, n_c, n_gt = (2 * GLA_DK // tn, GLA_DV // tn, GLA_DV // tn, D_MODEL // tn,
                                 2 * D_MODEL // tn)
    e_qk = n_qk
    e_v = e_qk + n_v
    e_g = e_v + n_g
    e_c = e_g + n_c
    steps = e_c + n_gt

    def park(j, lo, cnt):
        return jnp.clip(j - lo, 0, cnt - 1)

    return pl.pallas_call(
        functools.partial(_proj_kernel, bounds=(e_qk, e_v, e_g, e_c)),
        out_shape=(jax.ShapeDtypeStruct((n, 2 * GLA_DK), F32),
                   jax.ShapeDtypeStruct((n, GLA_DV), BF16),
                   jax.ShapeDtypeStruct((n, GLA_DV), BF16),
                   jax.ShapeDtypeStruct((n, D_MODEL), F32),
                   jax.ShapeDtypeStruct((n, 2 * D_MODEL), BF16),
                   jax.ShapeDtypeStruct((n, R_PAD), F32)),
        grid=(n // tm, steps),
        in_specs=[
            pl.BlockSpec((tm, D_MODEL), lambda i, j: (i, 0)),
            pl.BlockSpec((1, D_MODEL), lambda i, j: (0, 0)),
            pl.BlockSpec((D_MODEL, tn), lambda i, j: (0, jnp.where(j < e_c, j, j + n_c))),
            pl.BlockSpec((D_MODEL, tn), lambda i, j: (0, e_c + park(j, e_g, n_c))),
            pl.BlockSpec((D_MODEL, R_PAD), lambda i, j: (0, 0)),
            pl.BlockSpec((1, tn), lambda i, j: (0, park(j, e_c, n_gt))),
        ],
        out_specs=(pl.BlockSpec((tm, tn), lambda i, j: (i, park(j, 0, n_qk))),
                   pl.BlockSpec((tm, tn), lambda i, j: (i, park(j, e_qk, n_v))),
                   pl.BlockSpec((tm, tn), lambda i, j: (i, park(j, e_v, n_g))),
                   pl.BlockSpec((tm, tn), lambda i, j: (i, park(j, e_g, n_c))),
                   pl.BlockSpec((tm, tn), lambda i, j: (i, park(j, e_c, n_gt))),
                   pl.BlockSpec((tm, R_PAD), lambda i, j: (i, 0))),
        scratch_shapes=[pltpu.VMEM((tm, D_MODEL), BF16)],
        compiler_params=_params(("parallel", "arbitrary")),
        name="proj",
    )(x, norm_g, w_z, w_z, w_r, gate_b)


def _norm_matmul_kernel(x_ref, g_ref, w_ref, o_ref):
    h = _rms(x_ref[...], g_ref[...]).astype(BF16)
    o_ref[...] = _dot(h, w_ref[...])


def _norm_matmul(x, norm_g, w, *, tm):
    n = x.shape[0]
    nout = w.shape[1]
    return pl.pallas_call(
        _norm_matmul_kernel,
        out_shape=jax.ShapeDtypeStruct((n, nout), F32),
        grid=(n // tm,),
        in_specs=[
            pl.BlockSpec((tm, D_MODEL), lambda i: (i, 0)),
            pl.BlockSpec((1, D_MODEL), lambda i: (0, 0)),
            pl.BlockSpec((D_MODEL, nout), lambda i: (0, 0)),
        ],
        out_specs=pl.BlockSpec((tm, nout), lambda i: (i, 0)),
        compiler_params=_params(("parallel",)),
        name="norm_matmul",
    )(x, norm_g, w)


def _split3(x):
    hi = x.astype(BF16)
    r1 = x - hi.astype(F32)
    mid = r1.astype(BF16)
    lo = (r1 - mid.astype(F32)).astype(BF16)
    return hi, mid, lo


def _gla_kernel(q_ref, k_ref, v_ref, sg_ref, r_ref, w2_ref, gb_ref, gn_ref, o_ref, st_ref, bc_ref,
                *, t_blk):
    @pl.when(pl.program_id(2) == 0)
    def _():
        st_ref[...] = jnp.zeros_like(st_ref)

    n_sub = CHUNK // SUB
    row = lax.broadcasted_iota(jnp.int32, (CHUNK, CHUNK), 0)
    col = lax.broadcasted_iota(jnp.int32, (CHUNK, CHUNK), 1)
    tri = (row >= col).astype(BF16)
    causal = col <= row

    pre = _dot(r_ref[...].astype(BF16), w2_ref[...]) + gb_ref[...]
    la = (jnp.minimum(pre, 0.0) - jnp.log1p(jnp.exp(-jnp.abs(pre)))) * (1.0 / GLA_TAU)
    hi, mid, lo = _split3(la)
    for c in range(t_blk // CHUNK):
        rs = slice(c * CHUNK, (c + 1) * CHUNK)
        bc_ref[rs, :] = _dot(tri, hi[rs, :]) + _dot(tri, mid[rs, :]) + _dot(tri, lo[rs, :])

    col_sub = lax.broadcasted_iota(jnp.int32, (SUBLANES, CHUNK), 1)

    for c in range(t_blk // CHUNK):
        rows = pl.ds(c * CHUNK, CHUNK)
        q = q_ref[rows, :] * (GLA_HK ** -0.5)
        k = k_ref[rows, :]
        v = v_ref[rows, :]
        b = bc_ref[rows, :]
        b_last = b[CHUNK - 1:CHUNK, :]

        st = st_ref[...]
        o = _dot_nt((q * jnp.exp(b)).astype(BF16), st.astype(BF16))
        kd = (k * jnp.exp(b_last - b)).astype(BF16)
        st_ref[...] = st * jnp.exp(b_last) + _dot_tn(v, kd)

        parts = [jnp.zeros((SUB, CHUNK), F32)]
        for i in range(1, n_sub):
            b_start = b[i * SUB - 1:i * SUB, :]
            rs = slice(i * SUB, (i + 1) * SUB)
            qi = (q[rs, :] * jnp.exp(b[rs, :] - b_start)).astype(BF16)
            ki = (k[:i * SUB, :] * jnp.exp(b_start - b[:i * SUB, :])).astype(BF16)
            ki = jnp.concatenate([ki, jnp.zeros((CHUNK - i * SUB, GLA_HK), BF16)], axis=0)
            parts.append(_dot_nt(qi, ki))
        p_off = jnp.concatenate(parts, axis=0)

        b2 = b * LOG2E
        diag = []
        for i in range(n_sub):
            rs = slice(i * SUB, (i + 1) * SUB)
            qs, ks, bs = q[rs, :], k[rs, :], b2[rs, :]
            q_hi, b_hi = qs[SUBLANES:, :], bs[SUBLANES:, :]
            p_lo = jnp.zeros((SUBLANES, CHUNK), F32)
            p_hi = jnp.zeros((SUBLANES, CHUNK), F32)
            for jl in range(SUB):
                kj = ks[jl:jl + 1, :]
                bj = bs[jl:jl + 1, :]
                hit = col_sub == i * SUB + jl
                if jl < SUBLANES:
                    s = jnp.sum(qs * kj * jnp.exp2(bs - bj), axis=-1, keepdims=True)
                    p_lo = jnp.where(hit, s[:SUBLANES, :], p_lo)
                    p_hi = jnp.where(hit, s[SUBLANES:, :], p_hi)
                else:
                    s = jnp.sum(q_hi * kj * jnp.exp2(b_hi - bj), axis=-1, keepdims=True)
                    p_hi = jnp.where(hit, s, p_hi)
            diag += [p_lo, p_hi]
        p_diag = jnp.concatenate(diag, axis=0)
        p = p_off + jnp.where(causal, p_diag, 0.0)
        o = o + _dot(p.astype(BF16), v)

        o = o * lax.rsqrt(jnp.mean(o * o, axis=-1, keepdims=True) + EPS) * gn_ref[...]
        o_ref[rows, :] = (o * sg_ref[rows, :].astype(F32)).astype(BF16)


def _gla(qk, v, sg, r, w2, gate_b, gn, *, batch, seq, t_blk=512):
    nt = seq // t_blk
    rowmap = lambda b, h, t: b * nt + t
    return pl.pallas_call(
        functools.partial(_gla_kernel, t_blk=t_blk),
        out_shape=jax.ShapeDtypeStruct((batch * seq, GLA_DV), BF16),
        grid=(batch, GLA_HEADS, nt),
        in_specs=[
            pl.BlockSpec((t_blk, GLA_HK), lambda b, h, t: (rowmap(b, h, t), h)),
            pl.BlockSpec((t_blk, GLA_HK), lambda b, h, t: (rowmap(b, h, t), GLA_HEADS + h)),
            pl.BlockSpec((t_blk, GLA_HV), lambda b, h, t: (rowmap(b, h, t), h)),
            pl.BlockSpec((t_blk, GLA_HV), lambda b, h, t: (rowmap(b, h, t), h)),
            pl.BlockSpec((t_blk, R_PAD), lambda b, h, t: (rowmap(b, h, t), 0)),
            pl.BlockSpec((R_PAD, GLA_HK), lambda b, h, t: (0, h)),
            pl.BlockSpec((1, GLA_HK), lambda b, h, t: (0, h)),
            pl.BlockSpec((1, GLA_HV), lambda b, h, t: (0, h)),
        ],
        out_specs=pl.BlockSpec((t_blk, GLA_HV), lambda b, h, t: (rowmap(b, h, t), h)),
        scratch_shapes=[pltpu.VMEM((GLA_HV, GLA_HK), F32), pltpu.VMEM((t_blk, GLA_HK), F32)],
        compiler_params=_params(("parallel", "parallel", "arbitrary")),
        name="gla",
    )(qk, qk, v, sg, r, w2, gate_b, gn)


def _conv_kernel(c_ref, cw_ref, cb_ref, lg_ref, lb_ref, o_ref, cext_ref, xs_ref, cv_ref,
                 *, tc, cblk, rblk):
    t = pl.program_id(1)

    @pl.when(t == 0)
    def _():
        cext_ref[0:CONV_HALO, :] = jnp.zeros((CONV_HALO, D_MODEL), F32)

    @pl.when(t > 0)
    def _():
        cext_ref[0:CONV_HALO, :] = cext_ref[tc:tc + CONV_HALO, :]

    cext_ref[CONV_HALO:CONV_HALO + tc, :] = c_ref[...]

    row0 = CONV_HALO - (CONV_K - 1)
    n_shift = tc + CONV_HALO - SUBLANES

    def col_body(cb, carry):
        cols = pl.ds(pl.multiple_of(cb * cblk, cblk), cblk)
        for s in range(1, SUBLANES):
            xs_ref[s, 0:n_shift, :] = cext_ref[pl.ds(s, n_shift), cols]
        w = cw_ref[:, cols]
        bias = cb_ref[:, cols]
        for rb in range(tc // rblk):
            acc = jnp.broadcast_to(bias, (rblk, cblk))
            for kk in range(CONV_K):
                s = (row0 + kk) % SUBLANES
                base = rb * rblk + (row0 + kk) - s
                if s == 0:
                    win = cext_ref[pl.ds(base, rblk), cols]
                else:
                    win = xs_ref[s, pl.ds(base, rblk), :]
                acc = acc + w[kk:kk + 1, :] * win
            cv_ref[pl.ds(rb * rblk, rblk), cols] = acc
        return carry

    lax.fori_loop(0, D_MODEL // cblk, col_body, 0)

    c = cv_ref[...]
    mu = jnp.mean(c, axis=-1, keepdims=True)
    xc = c - mu
    y = xc * lax.rsqrt(jnp.mean(xc * xc, axis=-1, keepdims=True) + EPS) * lg_ref[...] + lb_ref[...]
    o_ref[...] = _silu(y).astype(BF16)


def _conv(c, conv_w, conv_b, ln_g, ln_b, *, batch, seq, tc=256, cblk=256, rblk=64):
    nt = seq // tc
    return pl.pallas_call(
        functools.partial(_conv_kernel, tc=tc, cblk=cblk, rblk=rblk),
        out_shape=jax.ShapeDtypeStruct((batch * seq, D_MODEL), BF16),
        grid=(batch, nt),
        in_specs=[
            pl.BlockSpec((tc, D_MODEL), lambda b, t: (b * nt + t, 0)),
            pl.BlockSpec((CONV_K, D_MODEL), lambda b, t: (0, 0)),
            pl.BlockSpec((1, D_MODEL), lambda b, t: (0, 0)),
            pl.BlockSpec((1, D_MODEL), lambda b, t: (0, 0)),
            pl.BlockSpec((1, D_MODEL), lambda b, t: (0, 0)),
        ],
        out_specs=pl.BlockSpec((tc, D_MODEL), lambda b, t: (b * nt + t, 0)),
        scratch_shapes=[pltpu.VMEM((tc + CONV_HALO, D_MODEL), F32),
                        pltpu.VMEM((SUBLANES, tc + CONV_HALO, cblk), F32),
                        pltpu.VMEM((tc, D_MODEL), F32)],
        compiler_params=_params(("parallel", "arbitrary")),
        name="conv",
    )(c, conv_w, conv_b, ln_g, ln_b)


def _merge_kernel(x_ref, og_ref, cn_ref, g0_ref, g1_ref, wg_ref, wc_ref, wo_ref, o_ref):
    ya = _dot(og_ref[...], wg_ref[...])
    yb = _dot(cn_ref[...], wc_ref[...])
    m = g0_ref[...].astype(F32) * ya + g1_ref[...].astype(F32) * yb
    o_ref[...] = x_ref[...] + _dot(m.astype(BF16), wo_ref[...])


def _merge(x, og, cn, gates, w_gla, w_conv, w_out, *, tm=256):
    n = x.shape[0]
    row = lambda i: (i, 0)
    wspec = _resident((D_MODEL, D_MODEL), lambda i: (0, 0))
    return pl.pallas_call(
        _merge_kernel,
        out_shape=jax.ShapeDtypeStruct((n, D_MODEL), F32),
        grid=(n // tm,),
        in_specs=[
            pl.BlockSpec((tm, D_MODEL), row),
            pl.BlockSpec((tm, D_MODEL), row),
            pl.BlockSpec((tm, D_MODEL), row),
            pl.BlockSpec((tm, D_MODEL), lambda i: (i, 0)),
            pl.BlockSpec((tm, D_MODEL), lambda i: (i, 1)),
            wspec, wspec, wspec,
        ],
        out_specs=pl.BlockSpec((tm, D_MODEL), row),
        compiler_params=_params(("parallel",)),
        name="merge",
    )(x, og, cn, gates, gates, w_gla, w_conv, w_out)


def _xattn_kernel(x_ref, g_ref, wq_ref, k_ref, v_ref, wo_ref, o_ref):
    x = x_ref[...]
    h = _rms(x, g_ref[...]).astype(BF16)
    q = _dot(h, wq_ref[...]).astype(BF16)
    k = k_ref[...].astype(BF16)
    v = v_ref[...].astype(BF16)
    outs = []
    for hd in range(XA_HEADS):
        cs = slice(hd * XA_HEAD_DIM, (hd + 1) * XA_HEAD_DIM)
        s = _dot_nt(q[:, cs], k[:, cs]) * (XA_HEAD_DIM ** -0.5)
        e = jnp.exp(s - jnp.max(s, axis=-1, keepdims=True))
        p = e / jnp.sum(e, axis=-1, keepdims=True)
        outs.append(_dot(p.astype(BF16), v[:, cs]))
    o = jnp.concatenate(outs, axis=-1).astype(BF16)
    o_ref[...] = x + _dot(o, wo_ref[...])


def _xattn(x, kv, norm_g, w_q, w_out, *, batch, seq, n_mem, tm=512):
    nt = seq // tm
    return pl.pallas_call(
        _xattn_kernel,
        out_shape=jax.ShapeDtypeStruct((batch * seq, D_MODEL), F32),
        grid=(batch, nt),
        in_specs=[
            pl.BlockSpec((tm, D_MODEL), lambda b, t: (b * nt + t, 0)),
            pl.BlockSpec((1, D_MODEL), lambda b, t: (0, 0)),
            pl.BlockSpec((D_MODEL, XA_WIDTH), lambda b, t: (0, 0)),
            pl.BlockSpec((n_mem, XA_WIDTH), lambda b, t: (b, 0)),
            pl.BlockSpec((n_mem, XA_WIDTH), lambda b, t: (b, 1)),
            pl.BlockSpec((XA_WIDTH, D_MODEL), lambda b, t: (0, 0)),
        ],
        out_specs=pl.BlockSpec((tm, D_MODEL), lambda b, t: (b * nt + t, 0)),
        compiler_params=_params(("parallel", "parallel")),
        name="xattn",
    )(x, norm_g, w_q, kv, kv, w_out)


def _row(v):
    return v.reshape(1, -1)


def kernel(x, mem, ffn1_norm, ffn1_w_in, ffn1_w_out, mix_norm, mix_w_in, gla_gate_w2, gla_gate_b, gla_out_norm, gla_proj, conv_w, conv_b, conv_ln_g, conv_ln_b, conv_proj, branch_gate_b, mix_w_out, xa_norm, xa_mem_norm, xa_w_q, xa_w_kv, xa_w_out, ffn2_norm, ffn2_w_in, ffn2_w_out, final_norm):
    batch, seq, _ = x.shape
    n_mem = mem.shape[1]
    depth = ffn1_norm.shape[0]
    xs = x.reshape(batch * seq, D_MODEL)
    mems = mem.reshape(batch * n_mem, D_MODEL)
    r0 = 2 * GLA_DK + GLA_DV
    r1 = r0 + GLA_RANK
    fin = _row(final_norm)

    for l in range(depth):
        w_mix = mix_w_in[l]
        w_z = jnp.concatenate([w_mix[:, :r0], w_mix[:, r1:]], axis=1).astype(BF16)
        w_r = jnp.pad(w_mix[:, r0:r1], ((0, 0), (0, R_PAD - GLA_RANK))).astype(BF16)
        w2 = jnp.pad(gla_gate_w2[l], ((0, R_PAD - GLA_RANK), (0, 0))).astype(BF16)

        xs = _ffn(xs, _row(ffn1_norm[l]), _cast_layer(ffn1_w_in, l), _cast_layer(ffn1_w_out, l),
                  fin, final_norm=False)

        qk, v, sg, c, gates, r = _proj(xs, _row(mix_norm[l]), w_z, w_r, _row(branch_gate_b[l]))
        og = _gla(qk, v, sg, r, w2, _row(gla_gate_b[l]), _row(gla_out_norm[l]),
                  batch=batch, seq=seq)
        cn = _conv(c, conv_w[l], _row(conv_b[l]), _row(conv_ln_g[l]), _row(conv_ln_b[l]),
                   batch=batch, seq=seq)
        xs = _merge(xs, og, cn, gates, _cast_layer(gla_proj, l), _cast_layer(conv_proj, l),
                    _cast_layer(mix_w_out, l))

        kv = _norm_matmul(mems, _row(xa_mem_norm[l]), _cast_layer(xa_w_kv, l), tm=batch * n_mem)
        xs = _xattn(xs, kv, _row(xa_norm[l]), _cast_layer(xa_w_q, l), _cast_layer(xa_w_out, l),
                    batch=batch, seq=seq, n_mem=n_mem)

        xs = _ffn(xs, _row(ffn2_norm[l]), _cast_layer(ffn2_w_in, l), _cast_layer(ffn2_w_out, l),
                  fin, final_norm=(l == depth - 1))

    return xs.reshape(batch, seq, D_MODEL)
```

```python
import functools
import math

import jax
import jax.numpy as jnp
from jax import lax
from jax.experimental import pallas as pl
from jax.experimental.pallas import tpu as pltpu

F32 = jnp.float32
BF16 = jnp.bfloat16

D_MODEL = 2048
CHUNK = 64
SUB = 16
SUBLANES = 8
GLA_HEADS = 4
GLA_DK = D_MODEL // 2
GLA_DV = D_MODEL
GLA_HK = GLA_DK // GLA_HEADS
GLA_HV = GLA_DV // GLA_HEADS
GLA_RANK = 16
GLA_TAU = 16.0
CONV_K = 31
CONV_HALO = 32
XA_HEADS = 4
XA_HEAD_DIM = 128
XA_WIDTH = XA_HEADS * XA_HEAD_DIM
D_FF = 5632
FFN_RES = 0.5
EPS = 1e-6
LANE = 128
R_PAD = LANE
LOG2E = math.log2(math.e)

VMEM_LIMIT = 56 * 1024 * 1024


def _params(sem):
    return pltpu.CompilerParams(dimension_semantics=sem, vmem_limit_bytes=VMEM_LIMIT)


def _rms(x, g):
    return x * lax.rsqrt(jnp.mean(x * x, axis=-1, keepdims=True) + EPS) * g


def _silu(x):
    return x * jax.nn.sigmoid(x)


def _dot(a, b):
    return jnp.dot(a, b, preferred_element_type=F32)


def _dot_nt(a, b):
    return lax.dot_general(a, b, (((1,), (1,)), ((), ())), preferred_element_type=F32)


def _dot_tn(a, b):
    return lax.dot_general(a, b, (((0,), (0,)), ((), ())), preferred_element_type=F32)


def _resident(shape, index_map):
    return pl.BlockSpec(shape, index_map, pipeline_mode=pl.Buffered(1))


def _cast_kernel(w_ref, o_ref):
    o_ref[...] = w_ref[...].astype(BF16)


def _cast_layer(w, l, *, block_elems=1 << 21):
    _, r, c = w.shape
    tr = min(r, 1 << int(math.log2(block_elems // c)))
    while r % tr:
        tr //= 2
    return pl.pallas_call(
        _cast_kernel,
        out_shape=jax.ShapeDtypeStruct((r, c), BF16),
        grid=(r // tr,),
        in_specs=[pl.BlockSpec((None, tr, c), lambda i: (l, i, 0))],
        out_specs=pl.BlockSpec((tr, c), lambda i: (i, 0)),
        compiler_params=_params(("parallel",)),
        name="cast",
    )(w)


def _mix_w_kernel(w_ref, z_ref, r_ref, *, r0, r1):
    w = w_ref[...]
    z_ref[:, :r0] = w[:, :r0].astype(BF16)
    z_ref[:, r0:] = w[:, r1:].astype(BF16)
    pad = jnp.zeros((w.shape[0], R_PAD - (r1 - r0)), F32)
    r_ref[...] = jnp.concatenate([w[:, r0:r1], pad], axis=1).astype(BF16)


def _mix_weights(w, l, *, r0, r1, tr=64):
    _, r, c = w.shape
    return pl.pallas_call(
        functools.partial(_mix_w_kernel, r0=r0, r1=r1),
        out_shape=(jax.ShapeDtypeStruct((r, c - (r1 - r0)), BF16),
                   jax.ShapeDtypeStruct((r, R_PAD), BF16)),
        grid=(r // tr,),
        in_specs=[pl.BlockSpec((None, tr, c), lambda i: (l, i, 0))],
        out_specs=(pl.BlockSpec((tr, c - (r1 - r0)), lambda i: (i, 0)),
                   pl.BlockSpec((tr, R_PAD), lambda i: (i, 0))),
        compiler_params=_params(("parallel",)),
        name="mix_weights",
    )(w)


def _ffn_kernel(x_ref, g_ref, wa_ref, wb_ref, w2_ref, fg_ref, o_ref, h_ref, *, final_norm):
    j = pl.program_id(1)

    @pl.when(j == 0)
    def _():
        x = x_ref[...]
        h_ref[...] = _rms(x, g_ref[...]).astype(BF16)
        o_ref[...] = x

    h = h_ref[...]
    a = _dot(h, wa_ref[...])
    b = _dot(h, wb_ref[...])
    g = (_silu(a) * b * FFN_RES).astype(BF16)
    o_ref[...] += _dot(g, w2_ref[...])

    if final_norm:
        @pl.when(j == pl.num_programs(1) - 1)
        def _():
            o_ref[...] = _rms(o_ref[...], fg_ref[...])


def _ffn(x, norm_g, w_in, w_out, final_g, *, final_norm, tm=1024, tf=512):
    n = x.shape[0]
    nf = D_FF // tf
    return pl.pallas_call(
        functools.partial(_ffn_kernel, final_norm=final_norm),
        out_shape=jax.ShapeDtypeStruct((n, D_MODEL), F32),
        grid=(n // tm, nf),
        in_specs=[
            pl.BlockSpec((tm, D_MODEL), lambda i, j: (i, 0)),
            pl.BlockSpec((1, D_MODEL), lambda i, j: (0, 0)),
            pl.BlockSpec((D_MODEL, tf), lambda i, j: (0, j)),
            pl.BlockSpec((D_MODEL, tf), lambda i, j: (0, j + nf)),
            pl.BlockSpec((tf, D_MODEL), lambda i, j: (j, 0)),
            pl.BlockSpec((1, D_MODEL), lambda i, j: (0, 0)),
        ],
        out_specs=pl.BlockSpec((tm, D_MODEL), lambda i, j: (i, 0)),
        scratch_shapes=[pltpu.VMEM((tm, D_MODEL), BF16)],
        compiler_params=_params(("parallel", "arbitrary")),
        name="ffn",
    )(x, norm_g, w_in, w_in, w_out, final_g)


def _proj_kernel(x_ref, g_ref, wa_ref, wb_ref, wr_ref, gb_ref,
                 qk_ref, v_ref, sg_ref, c_ref, gt_ref, r_ref, h_ref, *, bounds):
    j = pl.program_id(1)
    e_qk, e_v, e_g, e_c = bounds

    @pl.when(j == 0)
    def _():
        h = _rms(x_ref[...], g_ref[...]).astype(BF16)
        h_ref[...] = h
        r_ref[...] = _dot(h, wr_ref[...])

    @pl.when(j < e_qk)
    def _():
        qk_ref[...] = _dot(h_ref[...], wa_ref[...])

    @pl.when((j >= e_qk) & (j < e_v))
    def _():
        v_ref[...] = _dot(h_ref[...], wa_ref[...]).astype(BF16)

    half = h_ref.shape[0] // 2
    halves = (pl.ds(0, half), pl.ds(half, half))

    @pl.when((j >= e_v) & (j < e_g))
    def _():
        for rs in halves:
            sg_ref[rs, :] = _silu(_dot(h_ref[rs, :], wa_ref[...])).astype(BF16)

    @pl.when((j >= e_g) & (j < e_c))
    def _():
        for rs in halves:
            h = h_ref[rs, :]
            c_ref[rs, :] = _dot(h, wa_ref[...]) * jax.nn.sigmoid(_dot(h, wb_ref[...]))

    @pl.when(j >= e_c)
    def _():
        for rs in halves:
            gt_ref[rs, :] = jax.nn.sigmoid(_dot(h_ref[rs, :], wa_ref[...])
                                           + gb_ref[...]).astype(BF16)


def _proj(x, norm_g, w_z, w_r, gate_b, *, tm=1024, tn=512):
    n = x.shape[0]
    n_qk, n_v, n_g, n_c, n_gt = (2 * GLA_DK // tn, GLA_DV // tn, GLA_DV // tn, D_MODEL // tn,
                                 2 * D_MODEL // tn)
    e_qk = n_qk
    e_v = e_qk + n_v
    e_g = e_v + n_g
    e_c = e_g + n_c
    steps = e_c + n_gt

    def park(j, lo, cnt):
        return jnp.clip(j - lo, 0, cnt - 1)

    return pl.pallas_call(
        functools.partial(_proj_kernel, bounds=(e_qk, e_v, e_g, e_c)),
        out_shape=(jax.ShapeDtypeStruct((n, 2 * GLA_DK), F32),
                   jax.ShapeDtypeStruct((n, GLA_DV), BF16),
                   jax.ShapeDtypeStruct((n, GLA_DV), BF16),
                   jax.ShapeDtypeStruct((n, D_MODEL), F32),
                   jax.ShapeDtypeStruct((n, 2 * D_MODEL), BF16),
                   jax.ShapeDtypeStruct((n, R_PAD), F32)),
        grid=(n // tm, steps),
        in_specs=[
            pl.BlockSpec((tm, D_MODEL), lambda i, j: (i, 0)),
            pl.BlockSpec((1, D_MODEL), lambda i, j: (0, 0)),
            pl.BlockSpec((D_MODEL, tn), lambda i, j: (0, jnp.where(j < e_c, j, j + n_c))),
            pl.BlockSpec((D_MODEL, tn), lambda i, j: (0, e_c + park(j, e_g, n_c))),
            pl.BlockSpec((D_MODEL, R_PAD), lambda i, j: (0, 0)),
            pl.BlockSpec((1, tn), lambda i, j: (0, park(j, e_c, n_gt))),
        ],
        out_specs=(pl.BlockSpec((tm, tn), lambda i, j: (i, park(j, 0, n_qk))),
                   pl.BlockSpec((tm, tn), lambda i, j: (i, park(j, e_qk, n_v))),
                   pl.BlockSpec((tm, tn), lambda i, j: (i, park(j, e_v, n_g))),
                   pl.BlockSpec((tm, tn), lambda i, j: (i, park(j, e_g, n_c))),
                   pl.BlockSpec((tm, tn), lambda i, j: (i, park(j, e_c, n_gt))),
                   pl.BlockSpec((tm, R_PAD), lambda i, j: (i, 0))),
        scratch_shapes=[pltpu.VMEM((tm, D_MODEL), BF16)],
        compiler_params=_params(("parallel", "arbitrary")),
        name="proj",
    )(x, norm_g, w_z, w_z, w_r, gate_b)


def _norm_matmul_kernel(x_ref, g_ref, w_ref, o_ref):
    h = _rms(x_ref[...], g_ref[...]).astype(BF16)
    o_ref[...] = _dot(h, w_ref[...])


def _norm_matmul(x, norm_g, w, *, tm):
    n = x.shape[0]
    nout = w.shape[1]
    return pl.pallas_call(
        _norm_matmul_kernel,
        out_shape=jax.ShapeDtypeStruct((n, nout), F32),
        grid=(n // tm,),
        in_specs=[
            pl.BlockSpec((tm, D_MODEL), lambda i: (i, 0)),
            pl.BlockSpec((1, D_MODEL), lambda i: (0, 0)),
            pl.BlockSpec((D_MODEL, nout), lambda i: (0, 0)),
        ],
        out_specs=pl.BlockSpec((tm, nout), lambda i: (i, 0)),
        compiler_params=_params(("parallel",)),
        name="norm_matmul",
    )(x, norm_g, w)


def _split3(x):
    hi = x.astype(BF16)
    r1 = x - hi.astype(F32)
    mid = r1.astype(BF16)
    lo = (r1 - mid.astype(F32)).astype(BF16)
    return hi, mid, lo


def _gla_kernel(q_ref, k_ref, v_ref, sg_ref, r_ref, w2_ref, gb_ref, gn_ref, o_ref, st_ref, bc_ref,
                *, t_blk):
    @pl.when(pl.program_id(2) == 0)
    def _():
        st_ref[...] = jnp.zeros_like(st_ref)

    n_sub = CHUNK // SUB
    row = lax.broadcasted_iota(jnp.int32, (CHUNK, CHUNK), 0)
    col = lax.broadcasted_iota(jnp.int32, (CHUNK, CHUNK), 1)
    tri = (row >= col).astype(BF16)
    causal = col <= row

    pre = _dot(r_ref[...].astype(BF16), w2_ref[...]) + gb_ref[...]
    la = (jnp.minimum(pre, 0.0) - jnp.log1p(jnp.exp(-jnp.abs(pre)))) * (1.0 / GLA_TAU)
    hi, mid, lo = _split3(la)
    for c in range(t_blk // CHUNK):
        rs = slice(c * CHUNK, (c + 1) * CHUNK)
        bc_ref[rs, :] = _dot(tri, hi[rs, :]) + _dot(tri, mid[rs, :]) + _dot(tri, lo[rs, :])

    col_sub = lax.broadcasted_iota(jnp.int32, (SUBLANES, CHUNK), 1)

    for c in range(t_blk // CHUNK):
        rows = pl.ds(c * CHUNK, CHUNK)
        q = q_ref[rows, :] * (GLA_HK ** -0.5)
        k = k_ref[rows, :]
        v = v_ref[rows, :]
        b = bc_ref[rows, :]
        b_last = b[CHUNK - 1:CHUNK, :]

        st = st_ref[...]
        o = _dot_nt((q * jnp.exp(b)).astype(BF16), st.astype(BF16))
        kd = (k * jnp.exp(b_last - b)).astype(BF16)
        st_ref[...] = st * jnp.exp(b_last) + _dot_tn(v, kd)

        parts = [jnp.zeros((SUB, CHUNK), F32)]
        for i in range(1, n_sub):
            b_start = b[i * SUB - 1:i * SUB, :]
            rs = slice(i * SUB, (i + 1) * SUB)
            qi = (q[rs, :] * jnp.exp(b[rs, :] - b_start)).astype(BF16)
            ki = (k[:i * SUB, :] * jnp.exp(b_start - b[:i * SUB, :])).astype(BF16)
            ki = jnp.concatenate([ki, jnp.zeros((CHUNK - i * SUB, GLA_HK), BF16)], axis=0)
            parts.append(_dot_nt(qi, ki))
        p_off = jnp.concatenate(parts, axis=0)

        b2 = b * LOG2E
        diag = []
        for i in range(n_sub):
            rs = slice(i * SUB, (i + 1) * SUB)
            qs, ks, bs = q[rs, :], k[rs, :], b2[rs, :]
            q_hi, b_hi = qs[SUBLANES:, :], bs[SUBLANES:, :]
            p_lo = jnp.zeros((SUBLANES, CHUNK), F32)
            p_hi = jnp.zeros((SUBLANES, CHUNK), F32)
            for jl in range(SUB):
                kj = ks[jl:jl + 1, :]
                bj = bs[jl:jl + 1, :]
                hit = col_sub == i * SUB + jl
                if jl < SUBLANES:
                    s = jnp.sum(qs * kj * jnp.exp2(bs - bj), axis=-1, keepdims=True)
                    p_lo = jnp.where(hit, s[:SUBLANES, :], p_lo)
                    p_hi = jnp.where(hit, s[SUBLANES:, :], p_hi)
                else:
                    s = jnp.sum(q_hi * kj * jnp.exp2(b_hi - bj), axis=-1, keepdims=True)
                    p_hi = jnp.where(hit, s, p_hi)
            diag += [p_lo, p_hi]
        p_diag = jnp.concatenate(diag, axis=0)
        p = p_off + jnp.where(causal, p_diag, 0.0)
        o = o + _dot(p.astype(BF16), v)

        o = o * lax.rsqrt(jnp.mean(o * o, axis=-1, keepdims=True) + EPS) * gn_ref[...]
        o_ref[rows, :] = (o * sg_ref[rows, :].astype(F32)).astype(BF16)


def _gla(qk, v, sg, r, w2, gate_b, gn, *, batch, seq, t_blk=512):
    nt = seq // t_blk
    rowmap = lambda b, h, t: b * nt + t
    return pl.pallas_call(
        functools.partial(_gla_kernel, t_blk=t_blk),
        out_shape=jax.ShapeDtypeStruct((batch * seq, GLA_DV), BF16),
        grid=(batch, GLA_HEADS, nt),
        in_specs=[
            pl.BlockSpec((t_blk, GLA_HK), lambda b, h, t: (rowmap(b, h, t), h)),
            pl.BlockSpec((t_blk, GLA_HK), lambda b, h, t: (rowmap(b, h, t), GLA_HEADS + h)),
            pl.BlockSpec((t_blk, GLA_HV), lambda b, h, t: (rowmap(b, h, t), h)),
            pl.BlockSpec((t_blk, GLA_HV), lambda b, h, t: (rowmap(b, h, t), h)),
            pl.BlockSpec((t_blk, R_PAD), lambda b, h, t: (rowmap(b, h, t), 0)),
            pl.BlockSpec((R_PAD, GLA_HK), lambda b, h, t: (0, h)),
            pl.BlockSpec((1, GLA_HK), lambda b, h, t: (0, h)),
            pl.BlockSpec((1, GLA_HV), lambda b, h, t: (0, h)),
        ],
        out_specs=pl.BlockSpec((t_blk, GLA_HV), lambda b, h, t: (rowmap(b, h, t), h)),
        scratch_shapes=[pltpu.VMEM((GLA_HV, GLA_HK), F32), pltpu.VMEM((t_blk, GLA_HK), F32)],
        compiler_params=_params(("parallel", "parallel", "arbitrary")),
        name="gla",
    )(qk, qk, v, sg, r, w2, gate_b, gn)


def _conv_kernel(c_ref, cw_ref, cb_ref, lg_ref, lb_ref, o_ref, cext_ref, xs_ref, cv_ref,
                 *, tc, cblk, rblk):
    t = pl.program_id(1)

    @pl.when(t == 0)
    def _():
        cext_ref[0:CONV_HALO, :] = jnp.zeros((CONV_HALO, D_MODEL), F32)

    @pl.when(t > 0)
    def _():
        cext_ref[0:CONV_HALO, :] = cext_ref[tc:tc + CONV_HALO, :]

    cext_ref[CONV_HALO:CONV_HALO + tc, :] = c_ref[...]

    row0 = CONV_HALO - (CONV_K - 1)
    n_shift = tc + CONV_HALO - SUBLANES

    def col_body(cb, carry):
        cols = pl.ds(pl.multiple_of(cb * cblk, cblk), cblk)
        for s in range(1, SUBLANES):
            xs_ref[s, 0:n_shift, :] = cext_ref[pl.ds(s, n_shift), cols]
        w = cw_ref[:, cols]
        bias = cb_ref[:, cols]
        for rb in range(tc // rblk):
            acc = jnp.broadcast_to(bias, (rblk, cblk))
            for kk in range(CONV_K):
                s = (row0 + kk) % SUBLANES
                base = rb * rblk + (row0 + kk) - s
                if s == 0:
                    win = cext_ref[pl.ds(base, rblk), cols]
                else:
                    win = xs_ref[s, pl.ds(base, rblk), :]
                acc = acc + w[kk:kk + 1, :] * win
            cv_ref[pl.ds(rb * rblk, rblk), cols] = acc
        return carry

    lax.fori_loop(0, D_MODEL // cblk, col_body, 0)

    c = cv_ref[...]
    mu = jnp.mean(c, axis=-1, keepdims=True)
    xc = c - mu
    y = xc * lax.rsqrt(jnp.mean(xc * xc, axis=-1, keepdims=True) + EPS) * lg_ref[...] + lb_ref[...]
    o_ref[...] = _silu(y).astype(BF16)


def _conv(c, conv_w, conv_b, ln_g, ln_b, *, batch, seq, tc=256, cblk=256, rblk=64):
    nt = seq // tc
    return pl.pallas_call(
        functools.partial(_conv_kernel, tc=tc, cblk=cblk, rblk=rblk),
        out_shape=jax.ShapeDtypeStruct((batch * seq, D_MODEL), BF16),
        grid=(batch, nt),
        in_specs=[
            pl.BlockSpec((tc, D_MODEL), lambda b, t: (b * nt + t, 0)),
            pl.BlockSpec((CONV_K, D_MODEL), lambda b, t: (0, 0)),
            pl.BlockSpec((1, D_MODEL), lambda b, t: (0, 0)),
            pl.BlockSpec((1, D_MODEL), lambda b, t: (0, 0)),
            pl.BlockSpec((1, D_MODEL), lambda b, t: (0, 0)),
        ],
        out_specs=pl.BlockSpec((tc, D_MODEL), lambda b, t: (b * nt + t, 0)),
        scratch_shapes=[pltpu.VMEM((tc + CONV_HALO, D_MODEL), F32),
                        pltpu.VMEM((SUBLANES, tc + CONV_HALO, cblk), F32),
                        pltpu.VMEM((tc, D_MODEL), F32)],
        compiler_params=_params(("parallel", "arbitrary")),
        name="conv",
    )(c, conv_w, conv_b, ln_g, ln_b)


def _merge_kernel(x_ref, og_ref, cn_ref, g0_ref, g1_ref, wg_ref, wc_ref, wo_ref, o_ref):
    ya = _dot(og_ref[...], wg_ref[...])
    yb = _dot(cn_ref[...], wc_ref[...])
    m = g0_ref[...].astype(F32) * ya + g1_ref[...].astype(F32) * yb
    o_ref[...] = x_ref[...] + _dot(m.astype(BF16), wo_ref[...])


def _merge(x, og, cn, gates, w_gla, w_conv, w_out, *, tm=256):
    n = x.shape[0]
    row = lambda i: (i, 0)
    wspec = _resident((D_MODEL, D_MODEL), lambda i: (0, 0))
    return pl.pallas_call(
        _merge_kernel,
        out_shape=jax.ShapeDtypeStruct((n, D_MODEL), F32),
        grid=(n // tm,),
        in_specs=[
            pl.BlockSpec((tm, D_MODEL), row),
            pl.BlockSpec((tm, D_MODEL), row),
            pl.BlockSpec((tm, D_MODEL), row),
            pl.BlockSpec((tm, D_MODEL), lambda i: (i, 0)),
            pl.BlockSpec((tm, D_MODEL), lambda i: (i, 1)),
            wspec, wspec, wspec,
        ],
        out_specs=pl.BlockSpec((tm, D_MODEL), row),
        compiler_params=_params(("parallel",)),
        name="merge",
    )(x, og, cn, gates, gates, w_gla, w_conv, w_out)


def _xattn_kernel(x_ref, g_ref, wq_ref, k_ref, v_ref, wo_ref, o_ref):
    x = x_ref[...]
    h = _rms(x, g_ref[...]).astype(BF16)
    q = _dot(h, wq_ref[...]).astype(BF16)
    k = k_ref[...].astype(BF16)
    v = v_ref[...].astype(BF16)
    outs = []
    for hd in range(XA_HEADS):
        cs = slice(hd * XA_HEAD_DIM, (hd + 1) * XA_HEAD_DIM)
        s = _dot_nt(q[:, cs], k[:, cs]) * (XA_HEAD_DIM ** -0.5)
        e = jnp.exp(s - jnp.max(s, axis=-1, keepdims=True))
        p = e / jnp.sum(e, axis=-1, keepdims=True)
        outs.append(_dot(p.astype(BF16), v[:, cs]))
    o = jnp.concatenate(outs, axis=-1).astype(BF16)
    o_ref[...] = x + _dot(o, wo_ref[...])


def _xattn(x, kv, norm_g, w_q, w_out, *, batch, seq, n_mem, tm=512):
    nt = seq // tm
    return pl.pallas_call(
        _xattn_kernel,
        out_shape=jax.ShapeDtypeStruct((batch * seq, D_MODEL), F32),
        grid=(batch, nt),
        in_specs=[
            pl.BlockSpec((tm, D_MODEL), lambda b, t: (b * nt + t, 0)),
            pl.BlockSpec((1, D_MODEL), lambda b, t: (0, 0)),
            pl.BlockSpec((D_MODEL, XA_WIDTH), lambda b, t: (0, 0)),
            pl.BlockSpec((n_mem, XA_WIDTH), lambda b, t: (b, 0)),
            pl.BlockSpec((n_mem, XA_WIDTH), lambda b, t: (b, 1)),
            pl.BlockSpec((XA_WIDTH, D_MODEL), lambda b, t: (0, 0)),
        ],
        out_specs=pl.BlockSpec((tm, D_MODEL), lambda b, t: (b * nt + t, 0)),
        compiler_params=_params(("parallel", "parallel")),
        name="xattn",
    )(x, norm_g, w_q, kv, kv, w_out)


def _row(v):
    return v.reshape(1, -1)


def kernel(x, mem, ffn1_norm, ffn1_w_in, ffn1_w_out, mix_norm, mix_w_in, gla_gate_w2, gla_gate_b, gla_out_norm, gla_proj, conv_w, conv_b, conv_ln_g, conv_ln_b, conv_proj, branch_gate_b, mix_w_out, xa_norm, xa_mem_norm, xa_w_q, xa_w_kv, xa_w_out, ffn2_norm, ffn2_w_in, ffn2_w_out, final_norm):
    batch, seq, _ = x.shape
    n_mem = mem.shape[1]
    depth = ffn1_norm.shape[0]
    xs = x.reshape(batch * seq, D_MODEL)
    mems = mem.reshape(batch * n_mem, D_MODEL)
    r0 = 2 * GLA_DK + GLA_DV
    r1 = r0 + GLA_RANK
    fin = _row(final_norm)

    for l in range(depth):
        w_z, w_r = _mix_weights(mix_w_in, l, r0=r0, r1=r1)
        w2 = jnp.pad(gla_gate_w2[l], ((0, R_PAD - GLA_RANK), (0, 0))).astype(BF16)

        xs = _ffn(xs, _row(ffn1_norm[l]), _cast_layer(ffn1_w_in, l), _cast_layer(ffn1_w_out, l),
                  fin, final_norm=False)

        qk, v, sg, c, gates, r = _proj(xs, _row(mix_norm[l]), w_z, w_r, _row(branch_gate_b[l]))
        og = _gla(qk, v, sg, r, w2, _row(gla_gate_b[l]), _row(gla_out_norm[l]),
                  batch=batch, seq=seq)
        cn = _conv(c, conv_w[l], _row(conv_b[l]), _row(conv_ln_g[l]), _row(conv_ln_b[l]),
                   batch=batch, seq=seq)
        xs = _merge(xs, og, cn, gates, _cast_layer(gla_proj, l), _cast_layer(conv_proj, l),
                    _cast_layer(mix_w_out, l))

        kv = _norm_matmul(mems, _row(xa_mem_norm[l]), _cast_layer(xa_w_kv, l), tm=batch * n_mem)
        xs = _xattn(xs, kv, _row(xa_norm[l]), _cast_layer(xa_w_q, l), _cast_layer(xa_w_out, l),
                    batch=batch, seq=seq, n_mem=n_mem)

        xs = _ffn(xs, _row(ffn2_norm[l]), _cast_layer(ffn2_w_in, l), _cast_layer(ffn2_w_out, l),
                  fin, final_norm=(l == depth - 1))

    return xs.reshape(batch, seq, D_MODEL)
```

```python
import functools
import math

import jax
import jax.numpy as jnp
from jax import lax
from jax.experimental import pallas as pl
from jax.experimental.pallas import tpu as pltpu

F32 = jnp.float32
BF16 = jnp.bfloat16

D_MODEL = 2048
CHUNK = 64
SUB = 16
SUBLANES = 8
GLA_HEADS = 4
GLA_DK = D_MODEL // 2
GLA_DV = D_MODEL
GLA_HK = GLA_DK // GLA_HEADS
GLA_HV = GLA_DV // GLA_HEADS
GLA_RANK = 16
GLA_TAU = 16.0
GLA_MILD_LOG_DECAY = 4.0
CONV_K = 31
CONV_HALO = 32
XA_HEADS = 4
XA_HEAD_DIM = 128
XA_WIDTH = XA_HEADS * XA_HEAD_DIM
D_FF = 5632
FFN_RES = 0.5
EPS = 1e-6
LANE = 128
R_PAD = LANE
LOG2E = math.log2(math.e)

VMEM_LIMIT = 56 * 1024 * 1024


def _params(sem):
    return pltpu.CompilerParams(dimension_semantics=sem, vmem_limit_bytes=VMEM_LIMIT)


def _rms(x, g):
    return x * lax.rsqrt(jnp.mean(x * x, axis=-1, keepdims=True) + EPS) * g


def _silu(x):
    return x * jax.nn.sigmoid(x)


def _dot(a, b):
    return jnp.dot(a, b, preferred_element_type=F32)


def _dot_nt(a, b):
    return lax.dot_general(a, b, (((1,), (1,)), ((), ())), preferred_element_type=F32)


def _dot_tn(a, b):
    return lax.dot_general(a, b, (((0,), (0,)), ((), ())), preferred_element_type=F32)


def _resident(shape, index_map):
    return pl.BlockSpec(shape, index_map, pipeline_mode=pl.Buffered(1))


def _cast_kernel(w_ref, o_ref):
    o_ref[...] = w_ref[...].astype(BF16)


def _cast_layer(w, l, *, block_elems=1 << 21):
    _, r, c = w.shape
    tr = min(r, 1 << int(math.log2(block_elems // c)))
    while r % tr:
        tr //= 2
    return pl.pallas_call(
        _cast_kernel,
        out_shape=jax.ShapeDtypeStruct((r, c), BF16),
        grid=(r // tr,),
        in_specs=[pl.BlockSpec((None, tr, c), lambda i: (l, i, 0))],
        out_specs=pl.BlockSpec((tr, c), lambda i: (i, 0)),
        compiler_params=_params(("parallel",)),
        name="cast",
    )(w)


def _ffn_kernel(x_ref, g_ref, wa_ref, wb_ref, w2_ref, fg_ref, o_ref, h_ref, *, final_norm):
    j = pl.program_id(1)

    @pl.when(j == 0)
    def _():
        x = x_ref[...]
        h_ref[...] = _rms(x, g_ref[...]).astype(BF16)
        o_ref[...] = x

    h = h_ref[...]
    a = _dot(h, wa_ref[...])
    b = _dot(h, wb_ref[...])
    g = (_silu(a) * b * FFN_RES).astype(BF16)
    o_ref[...] += _dot(g, w2_ref[...])

    if final_norm:
        @pl.when(j == pl.num_programs(1) - 1)
        def _():
            o_ref[...] = _rms(o_ref[...], fg_ref[...])


def _ffn(x, norm_g, w_in, w_out, final_g, *, final_norm, tm=1024, tf=512):
    n = x.shape[0]
    nf = D_FF // tf
    return pl.pallas_call(
        functools.partial(_ffn_kernel, final_norm=final_norm),
        out_shape=jax.ShapeDtypeStruct((n, D_MODEL), F32),
        grid=(n // tm, nf),
        in_specs=[
            pl.BlockSpec((tm, D_MODEL), lambda i, j: (i, 0)),
            pl.BlockSpec((1, D_MODEL), lambda i, j: (0, 0)),
            pl.BlockSpec((D_MODEL, tf), lambda i, j: (0, j)),
            pl.BlockSpec((D_MODEL, tf), lambda i, j: (0, j + nf)),
            pl.BlockSpec((tf, D_MODEL), lambda i, j: (j, 0)),
            pl.BlockSpec((1, D_MODEL), lambda i, j: (0, 0)),
        ],
        out_specs=pl.BlockSpec((tm, D_MODEL), lambda i, j: (i, 0)),
        scratch_shapes=[pltpu.VMEM((tm, D_MODEL), BF16)],
        compiler_params=_params(("parallel", "arbitrary")),
        name="ffn",
    )(x, norm_g, w_in, w_in, w_out, final_g)


def _proj_kernel(x_ref, g_ref, wa_ref, wb_ref, wr_ref, gb_ref,
                 qk_ref, v_ref, sg_ref, c_ref, gt_ref, r_ref, h_ref, *, bounds):
    j = pl.program_id(1)
    e_qk, e_v, e_g, e_c = bounds

    @pl.when(j == 0)
    def _():
        h = _rms(x_ref[...], g_ref[...]).astype(BF16)
        h_ref[...] = h
        r_ref[...] = _dot(h, wr_ref[...])

    @pl.when(j < e_qk)
    def _():
        qk_ref[...] = _dot(h_ref[...], wa_ref[...])

    @pl.when((j >= e_qk) & (j < e_v))
    def _():
        v_ref[...] = _dot(h_ref[...], wa_ref[...]).astype(BF16)

    half = h_ref.shape[0] // 2
    halves = (pl.ds(0, half), pl.ds(half, half))

    @pl.when((j >= e_v) & (j < e_g))
    def _():
        for rs in halves:
            sg_ref[rs, :] = _silu(_dot(h_ref[rs, :], wa_ref[...])).astype(BF16)

    @pl.when((j >= e_g) & (j < e_c))
    def _():
        for rs in halves:
            h = h_ref[rs, :]
            c_ref[rs, :] = _dot(h, wa_ref[...]) * jax.nn.sigmoid(_dot(h, wb_ref[...]))

    @pl.when(j >= e_c)
    def _():
        for rs in halves:
            gt_ref[rs, :] = jax.nn.sigmoid(_dot(h_ref[rs, :], wa_ref[...])
                                           + gb_ref[...]).astype(BF16)


def _proj(x, norm_g, w_z, w_r, gate_b, *, tm=1024, tn=512):
    n = x.shape[0]
    n_qk, n_v, n_g, n_c, n_gt = (2 * GLA_DK // tn, GLA_DV // tn, GLA_DV // tn, D_MODEL // tn,
                                 2 * D_MODEL // tn)
    e_qk = n_qk
    e_v = e_qk + n_v
    e_g = e_v + n_g
    e_c = e_g + n_c
    steps = e_c + n_gt

    def park(j, lo, cnt):
        return jnp.clip(j - lo, 0, cnt - 1)

    return pl.pallas_call(
        functools.partial(_proj_kernel, bounds=(e_qk, e_v, e_g, e_c)),
        out_shape=(jax.ShapeDtypeStruct((n, 2 * GLA_DK), F32),
                   jax.ShapeDtypeStruct((n, GLA_DV), BF16),
                   jax.ShapeDtypeStruct((n, GLA_DV), BF16),
                   jax.ShapeDtypeStruct((n, D_MODEL), F32),
                   jax.ShapeDtypeStruct((n, 2 * D_MODEL), BF16),
                   jax.ShapeDtypeStruct((n, R_PAD), F32)),
        grid=(n // tm, steps),
        in_specs=[
            pl.BlockSpec((tm, D_MODEL), lambda i, j: (i, 0)),
            pl.BlockSpec((1, D_MODEL), lambda i, j: (0, 0)),
            pl.BlockSpec((D_MODEL, tn), lambda i, j: (0, jnp.where(j < e_c, j, j + n_c))),
            pl.BlockSpec((D_MODEL, tn), lambda i, j: (0, e_c + park(j, e_g, n_c))),
            pl.BlockSpec((D_MODEL, R_PAD), lambda i, j: (0, 0)),
            pl.BlockSpec((1, tn), lambda i, j: (0, park(j, e_c, n_gt))),
        ],
        out_specs=(pl.BlockSpec((tm, tn), lambda i, j: (i, park(j, 0, n_qk))),
                   pl.BlockSpec((tm, tn), lambda i, j: (i, park(j, e_qk, n_v))),
                   pl.BlockSpec((tm, tn), lambda i, j: (i, park(j, e_v, n_g))),
                   pl.BlockSpec((tm, tn), lambda i, j: (i, park(j, e_g, n_c))),
                   pl.BlockSpec((tm, tn), lambda i, j: (i, park(j, e_c, n_gt))),
                   pl.BlockSpec((tm, R_PAD), lambda i, j: (i, 0))),
        scratch_shapes=[pltpu.VMEM((tm, D_MODEL), BF16)],
        compiler_params=_params(("parallel", "arbitrary")),
        name="proj",
    )(x, norm_g, w_z, w_z, w_r, gate_b)


def _norm_matmul_kernel(x_ref, g_ref, w_ref, o_ref):
    h = _rms(x_ref[...], g_ref[...]).astype(BF16)
    o_ref[...] = _dot(h, w_ref[...])


def _norm_matmul(x, norm_g, w, *, tm):
    n = x.shape[0]
    nout = w.shape[1]
    return pl.pallas_call(
        _norm_matmul_kernel,
        out_shape=jax.ShapeDtypeStruct((n, nout), F32),
        grid=(n // tm,),
        in_specs=[
            pl.BlockSpec((tm, D_MODEL), lambda i: (i, 0)),
            pl.BlockSpec((1, D_MODEL), lambda i: (0, 0)),
            pl.BlockSpec((D_MODEL, nout), lambda i: (0, 0)),
        ],
        out_specs=pl.BlockSpec((tm, nout), lambda i: (i, 0)),
        compiler_params=_params(("parallel",)),
        name="norm_matmul",
    )(x, norm_g, w)


def _split3(x):
    hi = x.astype(BF16)
    r1 = x - hi.astype(F32)
    mid = r1.astype(BF16)
    lo = (r1 - mid.astype(F32)).astype(BF16)
    return hi, mid, lo


def _gla_kernel(q_ref, k_ref, v_ref, sg_ref, r_ref, w2_ref, gb_ref, gn_ref, o_ref, st_ref, bc_ref,
                *, t_blk):
    @pl.when(pl.program_id(2) == 0)
    def _():
        st_ref[...] = jnp.zeros_like(st_ref)

    n_sub = CHUNK // SUB
    row = lax.broadcasted_iota(jnp.int32, (CHUNK, CHUNK), 0)
    col = lax.broadcasted_iota(jnp.int32, (CHUNK, CHUNK), 1)
    tri = (row >= col).astype(BF16)
    causal = col <= row

    pre = _dot(r_ref[...].astype(BF16), w2_ref[...]) + gb_ref[...]
    la = (jnp.minimum(pre, 0.0) - jnp.log1p(jnp.exp(-jnp.abs(pre)))) * (1.0 / GLA_TAU)
    hi, mid, lo = _split3(la)
    for c in range(t_blk // CHUNK):
        rs = slice(c * CHUNK, (c + 1) * CHUNK)
        bc_ref[rs, :] = _dot(tri, hi[rs, :]) + _dot(tri, mid[rs, :]) + _dot(tri, lo[rs, :])

    col_sub = lax.broadcasted_iota(jnp.int32, (SUBLANES, CHUNK), 1)

    def chunk_loop(exact_diag):
        for c in range(t_blk // CHUNK):
            rows = pl.ds(c * CHUNK, CHUNK)
            q = q_ref[rows, :] * (GLA_HK ** -0.5)
            k = k_ref[rows, :]
            v = v_ref[rows, :]
            b = bc_ref[rows, :]
            b_last = b[CHUNK - 1:CHUNK, :]

            st = st_ref[...]
            o = _dot_nt((q * jnp.exp(b)).astype(BF16), st.astype(BF16))
            kd = (k * jnp.exp(b_last - b)).astype(BF16)
            st_ref[...] = st * jnp.exp(b_last) + _dot_tn(v, kd)

            parts = []
            for i in range(n_sub):
                rs = slice(i * SUB, (i + 1) * SUB)
                n_keys = (i if exact_diag else i + 1) * SUB
                if n_keys == 0:
                    parts.append(jnp.zeros((SUB, CHUNK), F32))
                    continue
                if i == 0:
                    qi = q[rs, :] * jnp.exp(b[rs, :])
                    ki = k[:n_keys, :] * jnp.exp(-b[:n_keys, :])
                else:
                    b_start = b[i * SUB - 1:i * SUB, :]
                    qi = q[rs, :] * jnp.exp(b[rs, :] - b_start)
                    ki = k[:n_keys, :] * jnp.exp(b_start - b[:n_keys, :])
                ki = ki.astype(BF16)
                if n_keys < CHUNK:
                    ki = jnp.concatenate([ki, jnp.zeros((CHUNK - n_keys, GLA_HK), BF16)], axis=0)
                parts.append(_dot_nt(qi.astype(BF16), ki))
            p = jnp.concatenate(parts, axis=0)

            if exact_diag:
                b2 = b * LOG2E
                diag = []
                for i in range(n_sub):
                    rs = slice(i * SUB, (i + 1) * SUB)
                    qs, ks, bs = q[rs, :], k[rs, :], b2[rs, :]
                    q_hi, b_hi = qs[SUBLANES:, :], bs[SUBLANES:, :]
                    p_lo = jnp.zeros((SUBLANES, CHUNK), F32)
                    p_hi = jnp.zeros((SUBLANES, CHUNK), F32)
                    for jl in range(SUB):
                        kj = ks[jl:jl + 1, :]
                        bj = bs[jl:jl + 1, :]
                        hit = col_sub == i * SUB + jl
                        if jl < SUBLANES:
                            s = jnp.sum(qs * kj * jnp.exp2(bs - bj), axis=-1, keepdims=True)
                            p_lo = jnp.where(hit, s[:SUBLANES, :], p_lo)
                            p_hi = jnp.where(hit, s[SUBLANES:, :], p_hi)
                        else:
                            s = jnp.sum(q_hi * kj * jnp.exp2(b_hi - bj), axis=-1, keepdims=True)
                            p_hi = jnp.where(hit, s, p_hi)
                    diag += [p_lo, p_hi]
                p = p + jnp.where(causal, jnp.concatenate(diag, axis=0), 0.0)
            else:
                p = jnp.where(causal, p, 0.0)
            o = o + _dot(p.astype(BF16), v)

            o = o * lax.rsqrt(jnp.mean(o * o, axis=-1, keepdims=True) + EPS) * gn_ref[...]
            o_ref[rows, :] = (o * sg_ref[rows, :].astype(F32)).astype(BF16)

    mild = jnp.min(la) >= -GLA_MILD_LOG_DECAY

    @pl.when(mild)
    def _():
        chunk_loop(exact_diag=False)

    @pl.when(jnp.logical_not(mild))
    def _():
        chunk_loop(exact_diag=True)


def _gla(qk, v, sg, r, w2, gate_b, gn, *, batch, seq, t_blk=512):
    nt = seq // t_blk
    rowmap = lambda b, h, t: b * nt + t
    return pl.pallas_call(
        functools.partial(_gla_kernel, t_blk=t_blk),
        out_shape=jax.ShapeDtypeStruct((batch * seq, GLA_DV), BF16),
        grid=(batch, GLA_HEADS, nt),
        in_specs=[
            pl.BlockSpec((t_blk, GLA_HK), lambda b, h, t: (rowmap(b, h, t), h)),
            pl.BlockSpec((t_blk, GLA_HK), lambda b, h, t: (rowmap(b, h, t), GLA_HEADS + h)),
            pl.BlockSpec((t_blk, GLA_HV), lambda b, h, t: (rowmap(b, h, t), h)),
            pl.BlockSpec((t_blk, GLA_HV), lambda b, h, t: (rowmap(b, h, t), h)),
            pl.BlockSpec((t_blk, R_PAD), lambda b, h, t: (rowmap(b, h, t), 0)),
            pl.BlockSpec((R_PAD, GLA_HK), lambda b, h, t: (0, h)),
            pl.BlockSpec((1, GLA_HK), lambda b, h, t: (0, h)),
            pl.BlockSpec((1, GLA_HV), lambda b, h, t: (0, h)),
        ],
        out_specs=pl.BlockSpec((t_blk, GLA_HV), lambda b, h, t: (rowmap(b, h, t), h)),
        scratch_shapes=[pltpu.VMEM((GLA_HV, GLA_HK), F32), pltpu.VMEM((t_blk, GLA_HK), F32)],
        compiler_params=_params(("parallel", "parallel", "arbitrary")),
        name="gla",
    )(qk, qk, v, sg, r, w2, gate_b, gn)


def _conv_kernel(c_ref, cw_ref, cb_ref, lg_ref, lb_ref, o_ref, cext_ref, xs_ref, cv_ref,
                 *, tc, cblk, rblk):
    t = pl.program_id(1)

    @pl.when(t == 0)
    def _():
        cext_ref[0:CONV_HALO, :] = jnp.zeros((CONV_HALO, D_MODEL), F32)

    @pl.when(t > 0)
    def _():
        cext_ref[0:CONV_HALO, :] = cext_ref[tc:tc + CONV_HALO, :]

    cext_ref[CONV_HALO:CONV_HALO + tc, :] = c_ref[...]

    row0 = CONV_HALO - (CONV_K - 1)
    n_shift = tc + CONV_HALO - SUBLANES

    def col_body(cb, carry):
        cols = pl.ds(pl.multiple_of(cb * cblk, cblk), cblk)
        for s in range(1, SUBLANES):
            xs_ref[s, 0:n_shift, :] = cext_ref[pl.ds(s, n_shift), cols]
        w = cw_ref[:, cols]
        bias = cb_ref[:, cols]
        for rb in range(tc // rblk):
            acc = jnp.broadcast_to(bias, (rblk, cblk))
            for kk in range(CONV_K):
                s = (row0 + kk) % SUBLANES
                base = rb * rblk + (row0 + kk) - s
                if s == 0:
                    win = cext_ref[pl.ds(base, rblk), cols]
                else:
                    win = xs_ref[s, pl.ds(base, rblk), :]
                acc = acc + w[kk:kk + 1, :] * win
            cv_ref[pl.ds(rb * rblk, rblk), cols] = acc
        return carry

    lax.fori_loop(0, D_MODEL // cblk, col_body, 0)

    c = cv_ref[...]
    mu = jnp.mean(c, axis=-1, keepdims=True)
    xc = c - mu
    y = xc * lax.rsqrt(jnp.mean(xc * xc, axis=-1, keepdims=True) + EPS) * lg_ref[...] + lb_ref[...]
    o_ref[...] = _silu(y).astype(BF16)


def _conv(c, conv_w, conv_b, ln_g, ln_b, *, batch, seq, tc=256, cblk=256, rblk=64):
    nt = seq // tc
    return pl.pallas_call(
        functools.partial(_conv_kernel, tc=tc, cblk=cblk, rblk=rblk),
        out_shape=jax.ShapeDtypeStruct((batch * seq, D_MODEL), BF16),
        grid=(batch, nt),
        in_specs=[
            pl.BlockSpec((tc, D_MODEL), lambda b, t: (b * nt + t, 0)),
            pl.BlockSpec((CONV_K, D_MODEL), lambda b, t: (0, 0)),
            pl.BlockSpec((1, D_MODEL), lambda b, t: (0, 0)),
            pl.BlockSpec((1, D_MODEL), lambda b, t: (0, 0)),
            pl.BlockSpec((1, D_MODEL), lambda b, t: (0, 0)),
        ],
        out_specs=pl.BlockSpec((tc, D_MODEL), lambda b, t: (b * nt + t, 0)),
        scratch_shapes=[pltpu.VMEM((tc + CONV_HALO, D_MODEL), F32),
                        pltpu.VMEM((SUBLANES, tc + CONV_HALO, cblk), F32),
                        pltpu.VMEM((tc, D_MODEL), F32)],
        compiler_params=_params(("parallel", "arbitrary")),
        name="conv",
    )(c, conv_w, conv_b, ln_g, ln_b)


def _merge_kernel(x_ref, og_ref, cn_ref, g0_ref, g1_ref, wg_ref, wc_ref, wo_ref, o_ref):
    ya = _dot(og_ref[...], wg_ref[...])
    yb = _dot(cn_ref[...], wc_ref[...])
    m = g0_ref[...].astype(F32) * ya + g1_ref[...].astype(F32) * yb
    o_ref[...] = x_ref[...] + _dot(m.astype(BF16), wo_ref[...])


def _merge(x, og, cn, gates, w_gla, w_conv, w_out, *, tm=256):
    n = x.shape[0]
    row = lambda i: (i, 0)
    wspec = _resident((D_MODEL, D_MODEL), lambda i: (0, 0))
    return pl.pallas_call(
        _merge_kernel,
        out_shape=jax.ShapeDtypeStruct((n, D_MODEL), F32),
        grid=(n // tm,),
        in_specs=[
            pl.BlockSpec((tm, D_MODEL), row),
            pl.BlockSpec((tm, D_MODEL), row),
            pl.BlockSpec((tm, D_MODEL), row),
            pl.BlockSpec((tm, D_MODEL), lambda i: (i, 0)),
            pl.BlockSpec((tm, D_MODEL), lambda i: (i, 1)),
            wspec, wspec, wspec,
        ],
        out_specs=pl.BlockSpec((tm, D_MODEL), row),
        compiler_params=_params(("parallel",)),
        name="merge",
    )(x, og, cn, gates, gates, w_gla, w_conv, w_out)


def _xattn_kernel(x_ref, g_ref, wq_ref, k_ref, v_ref, wo_ref, o_ref):
    x = x_ref[...]
    h = _rms(x, g_ref[...]).astype(BF16)
    q = _dot(h, wq_ref[...]).astype(BF16)
    k = k_ref[...].astype(BF16)
    v = v_ref[...].astype(BF16)
    outs = []
    for hd in range(XA_HEADS):
        cs = slice(hd * XA_HEAD_DIM, (hd + 1) * XA_HEAD_DIM)
        s = _dot_nt(q[:, cs], k[:, cs]) * (XA_HEAD_DIM ** -0.5)
        e = jnp.exp(s - jnp.max(s, axis=-1, keepdims=True))
        p = e / jnp.sum(e, axis=-1, keepdims=True)
        outs.append(_dot(p.astype(BF16), v[:, cs]))
    o = jnp.concatenate(outs, axis=-1).astype(BF16)
    o_ref[...] = x + _dot(o, wo_ref[...])


def _xattn(x, kv, norm_g, w_q, w_out, *, batch, seq, n_mem, tm=512):
    nt = seq // tm
    return pl.pallas_call(
        _xattn_kernel,
        out_shape=jax.ShapeDtypeStruct((batch * seq, D_MODEL), F32),
        grid=(batch, nt),
        in_specs=[
            pl.BlockSpec((tm, D_MODEL), lambda b, t: (b * nt + t, 0)),
            pl.BlockSpec((1, D_MODEL), lambda b, t: (0, 0)),
            pl.BlockSpec((D_MODEL, XA_WIDTH), lambda b, t: (0, 0)),
            pl.BlockSpec((n_mem, XA_WIDTH), lambda b, t: (b, 0)),
            pl.BlockSpec((n_mem, XA_WIDTH), lambda b, t: (b, 1)),
            pl.BlockSpec((XA_WIDTH, D_MODEL), lambda b, t: (0, 0)),
        ],
        out_specs=pl.BlockSpec((tm, D_MODEL), lambda b, t: (b * nt + t, 0)),
        compiler_params=_params(("parallel", "parallel")),
        name="xattn",
    )(x, norm_g, w_q, kv, kv, w_out)


def _row(v):
    return v.reshape(1, -1)


def kernel(x, mem, ffn1_norm, ffn1_w_in, ffn1_w_out, mix_norm, mix_w_in, gla_gate_w2, gla_gate_b, gla_out_norm, gla_proj, conv_w, conv_b, conv_ln_g, conv_ln_b, conv_proj, branch_gate_b, mix_w_out, xa_norm, xa_mem_norm, xa_w_q, xa_w_kv, xa_w_out, ffn2_norm, ffn2_w_in, ffn2_w_out, final_norm):
    batch, seq, _ = x.shape
    n_mem = mem.shape[1]
    depth = ffn1_norm.shape[0]
    xs = x.reshape(batch * seq, D_MODEL)
    mems = mem.reshape(batch * n_mem, D_MODEL)
    r0 = 2 * GLA_DK + GLA_DV
    r1 = r0 + GLA_RANK
    fin = _row(final_norm)

    for l in range(depth):
        w_mix = mix_w_in[l]
        w_z = jnp.concatenate([w_mix[:, :r0], w_mix[:, r1:]], axis=1).astype(BF16)
        w_r = jnp.pad(w_mix[:, r0:r1], ((0, 0), (0, R_PAD - GLA_RANK))).astype(BF16)
        w2 = jnp.pad(gla_gate_w2[l], ((0, R_PAD - GLA_RANK), (0, 0))).astype(BF16)

        xs = _ffn(xs, _row(ffn1_norm[l]), _cast_layer(ffn1_w_in, l), _cast_layer(ffn1_w_out, l),
                  fin, final_norm=False)

        qk, v, sg, c, gates, r = _proj(xs, _row(mix_norm[l]), w_z, w_r, _row(branch_gate_b[l]))
        og = _gla(qk, v, sg, r, w2, _row(gla_gate_b[l]), _row(gla_out_norm[l]),
                  batch=batch, seq=seq)
        cn = _conv(c, conv_w[l], _row(conv_b[l]), _row(conv_ln_g[l]), _row(conv_ln_b[l]),
                   batch=batch, seq=seq)
        xs = _merge(xs, og, cn, gates, _cast_layer(gla_proj, l), _cast_layer(conv_proj, l),
                    _cast_layer(mix_w_out, l))

        kv = _norm_matmul(mems, _row(xa_mem_norm[l]), _cast_layer(xa_w_kv, l), tm=batch * n_mem)
        xs = _xattn(xs, kv, _row(xa_norm[l]), _cast_layer(xa_w_q, l), _cast_layer(xa_w_out, l),
                    batch=batch, seq=seq, n_mem=n_mem)

        xs = _ffn(xs, _row(ffn2_norm[l]), _cast_layer(ffn2_w_in, l), _cast_layer(ffn2_w_out, l),
                  fin, final_norm=(l == depth - 1))

    return xs.reshape(batch, seq, D_MODEL)
```

```python
import functools
import math

import jax
import jax.numpy as jnp
from jax import lax
from jax.experimental import pallas as pl
from jax.experimental.pallas import tpu as pltpu

F32 = jnp.float32
BF16 = jnp.bfloat16

D_MODEL = 2048
CHUNK = 64
SUB = 16
SUBLANES = 8
GLA_HEADS = 4
GLA_DK = D_MODEL // 2
GLA_DV = D_MODEL
GLA_HK = GLA_DK // GLA_HEADS
GLA_HV = GLA_DV // GLA_HEADS
GLA_RANK = 16
GLA_TAU = 16.0
GLA_MILD_LOG_DECAY = 4.0
CONV_K = 31
CONV_HALO = 32
XA_HEADS = 4
XA_HEAD_DIM = 128
XA_WIDTH = XA_HEADS * XA_HEAD_DIM
D_FF = 5632
FFN_RES = 0.5
EPS = 1e-6
LANE = 128
R_PAD = LANE
LOG2E = math.log2(math.e)

VMEM_LIMIT = 56 * 1024 * 1024


def _params(sem):
    return pltpu.CompilerParams(dimension_semantics=sem, vmem_limit_bytes=VMEM_LIMIT)


def _rms(x, g):
    return x * lax.rsqrt(jnp.mean(x * x, axis=-1, keepdims=True) + EPS) * g


def _silu(x):
    return x * jax.nn.sigmoid(x)


def _dot(a, b):
    return jnp.dot(a, b, preferred_element_type=F32)


def _dot_nt(a, b):
    return lax.dot_general(a, b, (((1,), (1,)), ((), ())), preferred_element_type=F32)


def _dot_tn(a, b):
    return lax.dot_general(a, b, (((0,), (0,)), ((), ())), preferred_element_type=F32)


def _resident(shape, index_map):
    return pl.BlockSpec(shape, index_map, pipeline_mode=pl.Buffered(1))


def _cast_kernel(w_ref, o_ref):
    o_ref[...] = w_ref[...].astype(BF16)


def _cast_layer(w, l, *, block_elems=1 << 21):
    _, r, c = w.shape
    tr = min(r, 1 << int(math.log2(block_elems // c)))
    while r % tr:
        tr //= 2
    return pl.pallas_call(
        _cast_kernel,
        out_shape=jax.ShapeDtypeStruct((r, c), BF16),
        grid=(r // tr,),
        in_specs=[pl.BlockSpec((None, tr, c), lambda i: (l, i, 0))],
        out_specs=pl.BlockSpec((tr, c), lambda i: (i, 0)),
        compiler_params=_params(("parallel",)),
        name="cast",
    )(w)


def _ffn_kernel(x_ref, g_ref, wa_ref, wb_ref, w2_ref, fg_ref, o_ref, h_ref, *, final_norm):
    j = pl.program_id(1)

    @pl.when(j == 0)
    def _():
        x = x_ref[...]
        h_ref[...] = _rms(x, g_ref[...]).astype(BF16)
        o_ref[...] = x

    h = h_ref[...]
    a = _dot(h, wa_ref[...])
    b = _dot(h, wb_ref[...])
    g = (_silu(a) * b * FFN_RES).astype(BF16)
    o_ref[...] += _dot(g, w2_ref[...])

    if final_norm:
        @pl.when(j == pl.num_programs(1) - 1)
        def _():
            o_ref[...] = _rms(o_ref[...], fg_ref[...])


def _ffn(x, norm_g, w_in, w_out, final_g, *, final_norm, tm=1024, tf=512):
    n = x.shape[0]
    nf = D_FF // tf
    return pl.pallas_call(
        functools.partial(_ffn_kernel, final_norm=final_norm),
        out_shape=jax.ShapeDtypeStruct((n, D_MODEL), F32),
        grid=(n // tm, nf),
        in_specs=[
            pl.BlockSpec((tm, D_MODEL), lambda i, j: (i, 0)),
            pl.BlockSpec((1, D_MODEL), lambda i, j: (0, 0)),
            pl.BlockSpec((D_MODEL, tf), lambda i, j: (0, j)),
            pl.BlockSpec((D_MODEL, tf), lambda i, j: (0, j + nf)),
            pl.BlockSpec((tf, D_MODEL), lambda i, j: (j, 0)),
            pl.BlockSpec((1, D_MODEL), lambda i, j: (0, 0)),
        ],
        out_specs=pl.BlockSpec((tm, D_MODEL), lambda i, j: (i, 0)),
        scratch_shapes=[pltpu.VMEM((tm, D_MODEL), BF16)],
        compiler_params=_params(("parallel", "arbitrary")),
        name="ffn",
    )(x, norm_g, w_in, w_in, w_out, final_g)


def _proj_kernel(x_ref, g_ref, wa_ref, wb_ref, wr_ref, gb_ref,
                 qk_ref, v_ref, sg_ref, c_ref, gt_ref, r_ref, h_ref, *, bounds):
    j = pl.program_id(1)
    e_qk, e_v, e_g, e_c = bounds

    @pl.when(j == 0)
    def _():
        h = _rms(x_ref[...], g_ref[...]).astype(BF16)
        h_ref[...] = h
        r_ref[...] = _dot(h, wr_ref[...])

    @pl.when(j < e_qk)
    def _():
        qk_ref[...] = _dot(h_ref[...], wa_ref[...])

    @pl.when((j >= e_qk) & (j < e_v))
    def _():
        v_ref[...] = _dot(h_ref[...], wa_ref[...]).astype(BF16)

    half = h_ref.shape[0] // 2
    halves = (pl.ds(0, half), pl.ds(half, half))

    @pl.when((j >= e_v) & (j < e_g))
    def _():
        for rs in halves:
            sg_ref[rs, :] = _silu(_dot(h_ref[rs, :], wa_ref[...])).astype(BF16)

    @pl.when((j >= e_g) & (j < e_c))
    def _():
        for rs in halves:
            h = h_ref[rs, :]
            c_ref[rs, :] = _dot(h, wa_ref[...]) * jax.nn.sigmoid(_dot(h, wb_ref[...]))

    @pl.when(j >= e_c)
    def _():
        for rs in halves:
            gt_ref[rs, :] = jax.nn.sigmoid(_dot(h_ref[rs, :], wa_ref[...])
                                           + gb_ref[...]).astype(BF16)


def _proj(x, norm_g, w_z, w_r, gate_b, *, tm=1024, tn=512):
    n = x.shape[0]
    n_qk, n_v, n_g, n_c, n_gt = (2 * GLA_DK // tn, GLA_DV // tn, GLA_DV // tn, D_MODEL // tn,
                                 2 * D_MODEL // tn)
    e_qk = n_qk
    e_v = e_qk + n_v
    e_g = e_v + n_g
    e_c = e_g + n_c
    steps = e_c + n_gt

    def park(j, lo, cnt):
        return jnp.clip(j - lo, 0, cnt - 1)

    return pl.pallas_call(
        functools.partial(_proj_kernel, bounds=(e_qk, e_v, e_g, e_c)),
        out_shape=(jax.ShapeDtypeStruct((n, 2 * GLA_DK), F32),
                   jax.ShapeDtypeStruct((n, GLA_DV), BF16),
                   jax.ShapeDtypeStruct((n, GLA_DV), BF16),
                   jax.ShapeDtypeStruct((n, D_MODEL), F32),
                   jax.ShapeDtypeStruct((n, 2 * D_MODEL), BF16),
                   jax.ShapeDtypeStruct((n, R_PAD), F32)),
        grid=(n // tm, steps),
        in_specs=[
            pl.BlockSpec((tm, D_MODEL), lambda i, j: (i, 0)),
            pl.BlockSpec((1, D_MODEL), lambda i, j: (0, 0)),
            pl.BlockSpec((D_MODEL, tn), lambda i, j: (0, jnp.where(j < e_c, j, j + n_c))),
            pl.BlockSpec((D_MODEL, tn), lambda i, j: (0, e_c + park(j, e_g, n_c))),
            pl.BlockSpec((D_MODEL, R_PAD), lambda i, j: (0, 0)),
            pl.BlockSpec((1, tn), lambda i, j: (0, park(j, e_c, n_gt))),
        ],
        out_specs=(pl.BlockSpec((tm, tn), lambda i, j: (i, park(j, 0, n_qk))),
                   pl.BlockSpec((tm, tn), lambda i, j: (i, park(j, e_qk, n_v))),
                   pl.BlockSpec((tm, tn), lambda i, j: (i, park(j, e_v, n_g))),
                   pl.BlockSpec((tm, tn), lambda i, j: (i, park(j, e_g, n_c))),
                   pl.BlockSpec((tm, tn), lambda i, j: (i, park(j, e_c, n_gt))),
                   pl.BlockSpec((tm, R_PAD), lambda i, j: (i, 0))),
        scratch_shapes=[pltpu.VMEM((tm, D_MODEL), BF16)],
        compiler_params=_params(("parallel", "arbitrary")),
        name="proj",
    )(x, norm_g, w_z, w_z, w_r, gate_b)


def _norm_matmul_kernel(x_ref, g_ref, w_ref, o_ref):
    h = _rms(x_ref[...], g_ref[...]).astype(BF16)
    o_ref[...] = _dot(h, w_ref[...])


def _norm_matmul(x, norm_g, w, *, tm):
    n = x.shape[0]
    nout = w.shape[1]
    return pl.pallas_call(
        _norm_matmul_kernel,
        out_shape=jax.ShapeDtypeStruct((n, nout), F32),
        grid=(n // tm,),
        in_specs=[
            pl.BlockSpec((tm, D_MODEL), lambda i: (i, 0)),
            pl.BlockSpec((1, D_MODEL), lambda i: (0, 0)),
            pl.BlockSpec((D_MODEL, nout), lambda i: (0, 0)),
        ],
        out_specs=pl.BlockSpec((tm, nout), lambda i: (i, 0)),
        compiler_params=_params(("parallel",)),
        name="norm_matmul",
    )(x, norm_g, w)


def _split3(x):
    hi = x.astype(BF16)
    r1 = x - hi.astype(F32)
    mid = r1.astype(BF16)
    lo = (r1 - mid.astype(F32)).astype(BF16)
    return hi, mid, lo


def _gla_kernel(q_ref, k_ref, v_ref, sg_ref, r_ref, w2_ref, gb_ref, gn_ref, o_ref, st_ref, bc_ref,
                *, t_blk):
    @pl.when(pl.program_id(2) == 0)
    def _():
        st_ref[...] = jnp.zeros_like(st_ref)

    n_sub = CHUNK // SUB
    row = lax.broadcasted_iota(jnp.int32, (CHUNK, CHUNK), 0)
    col = lax.broadcasted_iota(jnp.int32, (CHUNK, CHUNK), 1)
    tri = (row >= col).astype(BF16)
    causal = col <= row

    pre = _dot(r_ref[...].astype(BF16), w2_ref[...]) + gb_ref[...]
    la = (jnp.minimum(pre, 0.0) - jnp.log1p(jnp.exp(-jnp.abs(pre)))) * (1.0 / GLA_TAU)
    hi, mid, lo = _split3(la)
    for c in range(t_blk // CHUNK):
        rs = slice(c * CHUNK, (c + 1) * CHUNK)
        bc_ref[rs, :] = _dot(tri, hi[rs, :]) + _dot(tri, mid[rs, :]) + _dot(tri, lo[rs, :])

    col_sub = lax.broadcasted_iota(jnp.int32, (SUBLANES, CHUNK), 1)

    def chunk_loop(exact_diag):
        for c in range(t_blk // CHUNK):
            rows = pl.ds(c * CHUNK, CHUNK)
            q = q_ref[rows, :] * (GLA_HK ** -0.5)
            k = k_ref[rows, :]
            v = v_ref[rows, :]
            b = bc_ref[rows, :] * LOG2E
            b_last = b[CHUNK - 1:CHUNK, :]

            st = st_ref[...]
            o = _dot_nt((q * jnp.exp2(b)).astype(BF16), st.astype(BF16))
            kd = (k * jnp.exp2(b_last - b)).astype(BF16)
            st_ref[...] = st * jnp.exp2(b_last) + _dot_tn(v, kd)

            parts = []
            for i in range(n_sub):
                rs = slice(i * SUB, (i + 1) * SUB)
                n_keys = (i if exact_diag else i + 1) * SUB
                if n_keys == 0:
                    parts.append(jnp.zeros((SUB, CHUNK), F32))
                    continue
                if i == 0:
                    qi = q[rs, :] * jnp.exp2(b[rs, :])
                    ki = k[:n_keys, :] * jnp.exp2(-b[:n_keys, :])
                else:
                    b_start = b[i * SUB - 1:i * SUB, :]
                    qi = q[rs, :] * jnp.exp2(b[rs, :] - b_start)
                    ki = k[:n_keys, :] * jnp.exp2(b_start - b[:n_keys, :])
                ki = ki.astype(BF16)
                if n_keys < CHUNK:
                    ki = jnp.concatenate([ki, jnp.zeros((CHUNK - n_keys, GLA_HK), BF16)], axis=0)
                parts.append(_dot_nt(qi.astype(BF16), ki))
            p = jnp.concatenate(parts, axis=0)

            if exact_diag:
                diag = []
                for i in range(n_sub):
                    rs = slice(i * SUB, (i + 1) * SUB)
                    qs, ks, bs = q[rs, :], k[rs, :], b[rs, :]
                    q_hi, b_hi = qs[SUBLANES:, :], bs[SUBLANES:, :]
                    p_lo = jnp.zeros((SUBLANES, CHUNK), F32)
                    p_hi = jnp.zeros((SUBLANES, CHUNK), F32)
                    for jl in range(SUB):
                        kj = ks[jl:jl + 1, :]
                        bj = bs[jl:jl + 1, :]
                        hit = col_sub == i * SUB + jl
                        if jl < SUBLANES:
                            s = jnp.sum(qs * kj * jnp.exp2(bs - bj), axis=-1, keepdims=True)
                            p_lo = jnp.where(hit, s[:SUBLANES, :], p_lo)
                            p_hi = jnp.where(hit, s[SUBLANES:, :], p_hi)
                        else:
                            s = jnp.sum(q_hi * kj * jnp.exp2(b_hi - bj), axis=-1, keepdims=True)
                            p_hi = jnp.where(hit, s, p_hi)
                    diag += [p_lo, p_hi]
                p = p + jnp.where(causal, jnp.concatenate(diag, axis=0), 0.0)
            else:
                p = jnp.where(causal, p, 0.0)
            o = o + _dot(p.astype(BF16), v)

            o = o * lax.rsqrt(jnp.mean(o * o, axis=-1, keepdims=True) + EPS) * gn_ref[...]
            o_ref[rows, :] = (o * sg_ref[rows, :].astype(F32)).astype(BF16)

    mild = jnp.min(la) >= -GLA_MILD_LOG_DECAY

    @pl.when(mild)
    def _():
        chunk_loop(exact_diag=False)

    @pl.when(jnp.logical_not(mild))
    def _():
        chunk_loop(exact_diag=True)


def _gla(qk, v, sg, r, w2, gate_b, gn, *, batch, seq, t_blk=512):
    nt = seq // t_blk
    rowmap = lambda b, h, t: b * nt + t
    return pl.pallas_call(
        functools.partial(_gla_kernel, t_blk=t_blk),
        out_shape=jax.ShapeDtypeStruct((batch * seq, GLA_DV), BF16),
        grid=(batch, GLA_HEADS, nt),
        in_specs=[
            pl.BlockSpec((t_blk, GLA_HK), lambda b, h, t: (rowmap(b, h, t), h)),
            pl.BlockSpec((t_blk, GLA_HK), lambda b, h, t: (rowmap(b, h, t), GLA_HEADS + h)),
            pl.BlockSpec((t_blk, GLA_HV), lambda b, h, t: (rowmap(b, h, t), h)),
            pl.BlockSpec((t_blk, GLA_HV), lambda b, h, t: (rowmap(b, h, t), h)),
            pl.BlockSpec((t_blk, R_PAD), lambda b, h, t: (rowmap(b, h, t), 0)),
            pl.BlockSpec((R_PAD, GLA_HK), lambda b, h, t: (0, h)),
            pl.BlockSpec((1, GLA_HK), lambda b, h, t: (0, h)),
            pl.BlockSpec((1, GLA_HV), lambda b, h, t: (0, h)),
        ],
        out_specs=pl.BlockSpec((t_blk, GLA_HV), lambda b, h, t: (rowmap(b, h, t), h)),
        scratch_shapes=[pltpu.VMEM((GLA_HV, GLA_HK), F32), pltpu.VMEM((t_blk, GLA_HK), F32)],
        compiler_params=_params(("parallel", "parallel", "arbitrary")),
        name="gla",
    )(qk, qk, v, sg, r, w2, gate_b, gn)


def _conv_kernel(c_ref, cw_ref, cb_ref, o_ref, cext_ref, xs_ref, *, tc, cblk, rblk):
    t = pl.program_id(1)

    @pl.when(t == 0)
    def _():
        cext_ref[0:CONV_HALO, :] = jnp.zeros((CONV_HALO, D_MODEL), F32)

    @pl.when(t > 0)
    def _():
        cext_ref[0:CONV_HALO, :] = cext_ref[tc:tc + CONV_HALO, :]

    cext_ref[CONV_HALO:CONV_HALO + tc, :] = c_ref[...]

    row0 = CONV_HALO - (CONV_K - 1)
    n_shift = tc + CONV_HALO - SUBLANES

    def col_body(cb, carry):
        cols = pl.ds(pl.multiple_of(cb * cblk, cblk), cblk)
        for s in range(1, SUBLANES):
            xs_ref[s, 0:n_shift, :] = cext_ref[pl.ds(s, n_shift), cols]
        w = cw_ref[:, cols]
        bias = cb_ref[:, cols]
        for rb in range(tc // rblk):
            acc = jnp.broadcast_to(bias, (rblk, cblk))
            for kk in range(CONV_K):
                s = (row0 + kk) % SUBLANES
                base = rb * rblk + (row0 + kk) - s
                if s == 0:
                    win = cext_ref[pl.ds(base, rblk), cols]
                else:
                    win = xs_ref[s, pl.ds(base, rblk), :]
                acc = acc + w[kk:kk + 1, :] * win
            o_ref[pl.ds(rb * rblk, rblk), cols] = acc
        return carry

    lax.fori_loop(0, D_MODEL // cblk, col_body, 0)


def _conv(c, conv_w, conv_b, *, batch, seq, tc=256, cblk=256, rblk=64):
    nt = seq // tc
    return pl.pallas_call(
        functools.partial(_conv_kernel, tc=tc, cblk=cblk, rblk=rblk),
        out_shape=jax.ShapeDtypeStruct((batch * seq, D_MODEL), F32),
        grid=(batch, nt),
        in_specs=[
            pl.BlockSpec((tc, D_MODEL), lambda b, t: (b * nt + t, 0)),
            pl.BlockSpec((CONV_K, D_MODEL), lambda b, t: (0, 0)),
            pl.BlockSpec((1, D_MODEL), lambda b, t: (0, 0)),
        ],
        out_specs=pl.BlockSpec((tc, D_MODEL), lambda b, t: (b * nt + t, 0)),
        scratch_shapes=[pltpu.VMEM((tc + CONV_HALO, D_MODEL), F32),
                        pltpu.VMEM((SUBLANES, tc + CONV_HALO, cblk), F32)],
        compiler_params=_params(("parallel", "arbitrary")),
        name="conv",
    )(c, conv_w, conv_b)


def _merge_kernel(x_ref, og_ref, cv_ref, lg_ref, lb_ref, g0_ref, g1_ref, wg_ref, wc_ref, wo_ref,
                  o_ref):
    ya = _dot(og_ref[...], wg_ref[...])
    c = cv_ref[...]
    xc = c - jnp.mean(c, axis=-1, keepdims=True)
    y = xc * lax.rsqrt(jnp.mean(xc * xc, axis=-1, keepdims=True) + EPS) * lg_ref[...] + lb_ref[...]
    yb = _dot(_silu(y).astype(BF16), wc_ref[...])
    m = g0_ref[...].astype(F32) * ya + g1_ref[...].astype(F32) * yb
    o_ref[...] = x_ref[...] + _dot(m.astype(BF16), wo_ref[...])


def _merge(x, og, cv, ln_g, ln_b, gates, w_gla, w_conv, w_out, *, tm=256):
    n = x.shape[0]
    row = lambda i: (i, 0)
    const = lambda i: (0, 0)
    wspec = _resident((D_MODEL, D_MODEL), const)
    return pl.pallas_call(
        _merge_kernel,
        out_shape=jax.ShapeDtypeStruct((n, D_MODEL), F32),
        grid=(n // tm,),
        in_specs=[
            pl.BlockSpec((tm, D_MODEL), row),
            pl.BlockSpec((tm, D_MODEL), row),
            pl.BlockSpec((tm, D_MODEL), row),
            pl.BlockSpec((1, D_MODEL), const),
            pl.BlockSpec((1, D_MODEL), const),
            pl.BlockSpec((tm, D_MODEL), lambda i: (i, 0)),
            pl.BlockSpec((tm, D_MODEL), lambda i: (i, 1)),
            wspec, wspec, wspec,
        ],
        out_specs=pl.BlockSpec((tm, D_MODEL), row),
        compiler_params=_params(("parallel",)),
        name="merge",
    )(x, og, cv, ln_g, ln_b, gates, gates, w_gla, w_conv, w_out)


def _xattn_kernel(x_ref, g_ref, wq_ref, k_ref, v_ref, wo_ref, o_ref):
    x = x_ref[...]
    h = _rms(x, g_ref[...]).astype(BF16)
    q = _dot(h, wq_ref[...]).astype(BF16)
    k = k_ref[...].astype(BF16)
    v = v_ref[...].astype(BF16)
    outs = []
    for hd in range(XA_HEADS):
        cs = slice(hd * XA_HEAD_DIM, (hd + 1) * XA_HEAD_DIM)
        s = _dot_nt(q[:, cs], k[:, cs]) * (XA_HEAD_DIM ** -0.5)
        e = jnp.exp(s - jnp.max(s, axis=-1, keepdims=True))
        p = e / jnp.sum(e, axis=-1, keepdims=True)
        outs.append(_dot(p.astype(BF16), v[:, cs]))
    o = jnp.concatenate(outs, axis=-1).astype(BF16)
    o_ref[...] = x + _dot(o, wo_ref[...])


def _xattn(x, kv, norm_g, w_q, w_out, *, batch, seq, n_mem, tm=512):
    nt = seq // tm
    return pl.pallas_call(
        _xattn_kernel,
        out_shape=jax.ShapeDtypeStruct((batch * seq, D_MODEL), F32),
        grid=(batch, nt),
        in_specs=[
            pl.BlockSpec((tm, D_MODEL), lambda b, t: (b * nt + t, 0)),
            pl.BlockSpec((1, D_MODEL), lambda b, t: (0, 0)),
            pl.BlockSpec((D_MODEL, XA_WIDTH), lambda b, t: (0, 0)),
            pl.BlockSpec((n_mem, XA_WIDTH), lambda b, t: (b, 0)),
            pl.BlockSpec((n_mem, XA_WIDTH), lambda b, t: (b, 1)),
            pl.BlockSpec((XA_WIDTH, D_MODEL), lambda b, t: (0, 0)),
        ],
        out_specs=pl.BlockSpec((tm, D_MODEL), lambda b, t: (b * nt + t, 0)),
        compiler_params=_params(("parallel", "parallel")),
        name="xattn",
    )(x, norm_g, w_q, kv, kv, w_out)


def _row(v):
    return v.reshape(1, -1)


def kernel(x, mem, ffn1_norm, ffn1_w_in, ffn1_w_out, mix_norm, mix_w_in, gla_gate_w2, gla_gate_b, gla_out_norm, gla_proj, conv_w, conv_b, conv_ln_g, conv_ln_b, conv_proj, branch_gate_b, mix_w_out, xa_norm, xa_mem_norm, xa_w_q, xa_w_kv, xa_w_out, ffn2_norm, ffn2_w_in, ffn2_w_out, final_norm):
    batch, seq, _ = x.shape
    n_mem = mem.shape[1]
    depth = ffn1_norm.shape[0]
    xs = x.reshape(batch * seq, D_MODEL)
    mems = mem.reshape(batch * n_mem, D_MODEL)
    r0 = 2 * GLA_DK + GLA_DV
    r1 = r0 + GLA_RANK
    fin = _row(final_norm)

    for l in range(depth):
        w_mix = mix_w_in[l]
        w_z = jnp.concatenate([w_mix[:, :r0], w_mix[:, r1:]], axis=1).astype(BF16)
        w_r = jnp.pad(w_mix[:, r0:r1], ((0, 0), (0, R_PAD - GLA_RANK))).astype(BF16)
        w2 = jnp.pad(gla_gate_w2[l], ((0, R_PAD - GLA_RANK), (0, 0))).astype(BF16)

        xs = _ffn(xs, _row(ffn1_norm[l]), _cast_layer(ffn1_w_in, l), _cast_layer(ffn1_w_out, l),
                  fin, final_norm=False)

        qk, v, sg, c, gates, r = _proj(xs, _row(mix_norm[l]), w_z, w_r, _row(branch_gate_b[l]))
        og = _gla(qk, v, sg, r, w2, _row(gla_gate_b[l]), _row(gla_out_norm[l]),
                  batch=batch, seq=seq)
        cv = _conv(c, conv_w[l], _row(conv_b[l]), batch=batch, seq=seq)
        xs = _merge(xs, og, cv, _row(conv_ln_g[l]), _row(conv_ln_b[l]), gates,
                    _cast_layer(gla_proj, l), _cast_layer(conv_proj, l), _cast_layer(mix_w_out, l))

        kv = _norm_matmul(mems, _row(xa_mem_norm[l]), _cast_layer(xa_w_kv, l), tm=batch * n_mem)
        xs = _xattn(xs, kv, _row(xa_norm[l]), _cast_layer(xa_w_q, l), _cast_layer(xa_w_out, l),
                    batch=batch, seq=seq, n_mem=n_mem)

        xs = _ffn(xs, _row(ffn2_norm[l]), _cast_layer(ffn2_w_in, l), _cast_layer(ffn2_w_out, l),
                  fin, final_norm=(l == depth - 1))

    return xs.reshape(batch, seq, D_MODEL)
```

```python
import functools
import math

import jax
import jax.numpy as jnp
from jax import lax
from jax.experimental import pallas as pl
from jax.experimental.pallas import tpu as pltpu

F32 = jnp.float32
BF16 = jnp.bfloat16

D_MODEL = 2048
CHUNK = 64
SUB = 16
SUBLANES = 8
GLA_HEADS = 4
GLA_DK = D_MODEL // 2
GLA_DV = D_MODEL
GLA_HK = GLA_DK // GLA_HEADS
GLA_HV = GLA_DV // GLA_HEADS
GLA_RANK = 16
GLA_TAU = 16.0
GLA_MILD_LOG_DECAY = 4.0
CONV_K = 31
CONV_HALO = 32
XA_HEADS = 4
XA_HEAD_DIM = 128
XA_WIDTH = XA_HEADS * XA_HEAD_DIM
D_FF = 5632
FFN_RES = 0.5
EPS = 1e-6
LANE = 128
R_PAD = LANE
LOG2E = math.log2(math.e)

VMEM_LIMIT = 56 * 1024 * 1024


def _params(sem):
    return pltpu.CompilerParams(dimension_semantics=sem, vmem_limit_bytes=VMEM_LIMIT)


def _rms(x, g):
    return x * lax.rsqrt(jnp.mean(x * x, axis=-1, keepdims=True) + EPS) * g


def _silu(x):
    return x * jax.nn.sigmoid(x)


def _dot(a, b):
    return jnp.dot(a, b, preferred_element_type=F32)


def _dot_nt(a, b):
    return lax.dot_general(a, b, (((1,), (1,)), ((), ())), preferred_element_type=F32)


def _dot_tn(a, b):
    return lax.dot_general(a, b, (((0,), (0,)), ((), ())), preferred_element_type=F32)


def _resident(shape, index_map):
    return pl.BlockSpec(shape, index_map, pipeline_mode=pl.Buffered(1))


def _cast_kernel(w_ref, o_ref):
    o_ref[...] = w_ref[...].astype(BF16)


def _cast_layer(w, l, *, block_elems=1 << 21):
    _, r, c = w.shape
    tr = min(r, 1 << int(math.log2(block_elems // c)))
    while r % tr:
        tr //= 2
    return pl.pallas_call(
        _cast_kernel,
        out_shape=jax.ShapeDtypeStruct((r, c), BF16),
        grid=(r // tr,),
        in_specs=[pl.BlockSpec((None, tr, c), lambda i: (l, i, 0))],
        out_specs=pl.BlockSpec((tr, c), lambda i: (i, 0)),
        compiler_params=_params(("parallel",)),
        name="cast",
    )(w)


def _ffn_kernel(x_ref, g_ref, wa_ref, wb_ref, w2_ref, fg_ref, o_ref, h_ref, *, final_norm):
    j = pl.program_id(1)

    @pl.when(j == 0)
    def _():
        x = x_ref[...]
        h_ref[...] = _rms(x, g_ref[...]).astype(BF16)
        o_ref[...] = x

    h = h_ref[...]
    a = _dot(h, wa_ref[...])
    b = _dot(h, wb_ref[...])
    g = (_silu(a) * b * FFN_RES).astype(BF16)
    o_ref[...] += _dot(g, w2_ref[...])

    if final_norm:
        @pl.when(j == pl.num_programs(1) - 1)
        def _():
            o_ref[...] = _rms(o_ref[...], fg_ref[...])


def _ffn(x, norm_g, w_in, w_out, final_g, *, final_norm, tm=1024, tf=512):
    n = x.shape[0]
    nf = D_FF // tf
    return pl.pallas_call(
        functools.partial(_ffn_kernel, final_norm=final_norm),
        out_shape=jax.ShapeDtypeStruct((n, D_MODEL), F32),
        grid=(n // tm, nf),
        in_specs=[
            pl.BlockSpec((tm, D_MODEL), lambda i, j: (i, 0)),
            pl.BlockSpec((1, D_MODEL), lambda i, j: (0, 0)),
            pl.BlockSpec((D_MODEL, tf), lambda i, j: (0, j)),
            pl.BlockSpec((D_MODEL, tf), lambda i, j: (0, j + nf)),
            pl.BlockSpec((tf, D_MODEL), lambda i, j: (j, 0)),
            pl.BlockSpec((1, D_MODEL), lambda i, j: (0, 0)),
        ],
        out_specs=pl.BlockSpec((tm, D_MODEL), lambda i, j: (i, 0)),
        scratch_shapes=[pltpu.VMEM((tm, D_MODEL), BF16)],
        compiler_params=_params(("parallel", "arbitrary")),
        name="ffn",
    )(x, norm_g, w_in, w_in, w_out, final_g)


def _proj_kernel(x_ref, g_ref, wa_ref, wb_ref, wr_ref, gb_ref,
                 qk_ref, v_ref, sg_ref, c_ref, gt_ref, r_ref, h_ref, *, bounds):
    j = pl.program_id(1)
    e_qk, e_v, e_g, e_c = bounds

    @pl.when(j == 0)
    def _():
        h = _rms(x_ref[...], g_ref[...]).astype(BF16)
        h_ref[...] = h
        r_ref[...] = _dot(h, wr_ref[...])

    @pl.when(j < e_qk)
    def _():
        qk_ref[...] = _dot(h_ref[...], wa_ref[...])

    @pl.when((j >= e_qk) & (j < e_v))
    def _():
        v_ref[...] = _dot(h_ref[...], wa_ref[...]).astype(BF16)

    half = h_ref.shape[0] // 2
    halves = (pl.ds(0, half), pl.ds(half, half))

    @pl.when((j >= e_v) & (j < e_g))
    def _():
        for rs in halves:
            sg_ref[rs, :] = _silu(_dot(h_ref[rs, :], wa_ref[...])).astype(BF16)

    @pl.when((j >= e_g) & (j < e_c))
    def _():
        for rs in halves:
            h = h_ref[rs, :]
            c_ref[rs, :] = _dot(h, wa_ref[...]) * jax.nn.sigmoid(_dot(h, wb_ref[...]))

    @pl.when(j >= e_c)
    def _():
        for rs in halves:
            gt_ref[rs, :] = jax.nn.sigmoid(_dot(h_ref[rs, :], wa_ref[...])
                                           + gb_ref[...]).astype(BF16)


def _proj(x, norm_g, w_z, w_r, gate_b, *, tm=1024, tn=512):
    n = x.shape[0]
    n_qk, n_v, n_g, n_c, n_gt = (2 * GLA_DK // tn, GLA_DV // tn, GLA_DV // tn, D_MODEL // tn,
                                 2 * D_MODEL // tn)
    e_qk = n_qk
    e_v = e_qk + n_v
    e_g = e_v + n_g
    e_c = e_g + n_c
    steps = e_c + n_gt

    def park(j, lo, cnt):
        return jnp.clip(j - lo, 0, cnt - 1)

    return pl.pallas_call(
        functools.partial(_proj_kernel, bounds=(e_qk, e_v, e_g, e_c)),
        out_shape=(jax.ShapeDtypeStruct((n, 2 * GLA_DK), F32),
                   jax.ShapeDtypeStruct((n, GLA_DV), BF16),
                   jax.ShapeDtypeStruct((n, GLA_DV), BF16),
                   jax.ShapeDtypeStruct((n, D_MODEL), F32),
                   jax.ShapeDtypeStruct((n, 2 * D_MODEL), BF16),
                   jax.ShapeDtypeStruct((n, R_PAD), F32)),
        grid=(n // tm, steps),
        in_specs=[
            pl.BlockSpec((tm, D_MODEL), lambda i, j: (i, 0)),
            pl.BlockSpec((1, D_MODEL), lambda i, j: (0, 0)),
            pl.BlockSpec((D_MODEL, tn), lambda i, j: (0, jnp.where(j < e_c, j, j + n_c))),
            pl.BlockSpec((D_MODEL, tn), lambda i, j: (0, e_c + park(j, e_g, n_c))),
            pl.BlockSpec((D_MODEL, R_PAD), lambda i, j: (0, 0)),
            pl.BlockSpec((1, tn), lambda i, j: (0, park(j, e_c, n_gt))),
        ],
        out_specs=(pl.BlockSpec((tm, tn), lambda i, j: (i, park(j, 0, n_qk))),
                   pl.BlockSpec((tm, tn), lambda i, j: (i, park(j, e_qk, n_v))),
                   pl.BlockSpec((tm, tn), lambda i, j: (i, park(j, e_v, n_g))),
                   pl.BlockSpec((tm, tn), lambda i, j: (i, park(j, e_g, n_c))),
                   pl.BlockSpec((tm, tn), lambda i, j: (i, park(j, e_c, n_gt))),
                   pl.BlockSpec((tm, R_PAD), lambda i, j: (i, 0))),
        scratch_shapes=[pltpu.VMEM((tm, D_MODEL), BF16)],
        compiler_params=_params(("parallel", "arbitrary")),
        name="proj",
    )(x, norm_g, w_z, w_z, w_r, gate_b)


def _norm_matmul_kernel(x_ref, g_ref, w_ref, o_ref):
    h = _rms(x_ref[...], g_ref[...]).astype(BF16)
    o_ref[...] = _dot(h, w_ref[...])


def _norm_matmul(x, norm_g, w, *, tm):
    n = x.shape[0]
    nout = w.shape[1]
    return pl.pallas_call(
        _norm_matmul_kernel,
        out_shape=jax.ShapeDtypeStruct((n, nout), F32),
        grid=(n // tm,),
        in_specs=[
            pl.BlockSpec((tm, D_MODEL), lambda i: (i, 0)),
            pl.BlockSpec((1, D_MODEL), lambda i: (0, 0)),
            pl.BlockSpec((D_MODEL, nout), lambda i: (0, 0)),
        ],
        out_specs=pl.BlockSpec((tm, nout), lambda i: (i, 0)),
        compiler_params=_params(("parallel",)),
        name="norm_matmul",
    )(x, norm_g, w)


def _split3(x):
    hi = x.astype(BF16)
    r1 = x - hi.astype(F32)
    mid = r1.astype(BF16)
    lo = (r1 - mid.astype(F32)).astype(BF16)
    return hi, mid, lo


def _gla_kernel(q_ref, k_ref, v_ref, sg_ref, r_ref, w2_ref, gb_ref, gn_ref, o_ref, st_ref, bc_ref,
                *, t_blk):
    @pl.when(pl.program_id(2) == 0)
    def _():
        st_ref[...] = jnp.zeros_like(st_ref)

    n_sub = CHUNK // SUB
    row = lax.broadcasted_iota(jnp.int32, (CHUNK, CHUNK), 0)
    col = lax.broadcasted_iota(jnp.int32, (CHUNK, CHUNK), 1)
    tri = (row >= col).astype(BF16)
    causal = col <= row

    pre = _dot(r_ref[...].astype(BF16), w2_ref[...]) + gb_ref[...]
    la = (jnp.minimum(pre, 0.0) - jnp.log1p(jnp.exp(-jnp.abs(pre)))) * (1.0 / GLA_TAU)
    hi, mid, lo = _split3(la)
    for c in range(t_blk // CHUNK):
        rs = slice(c * CHUNK, (c + 1) * CHUNK)
        bc_ref[rs, :] = _dot(tri, hi[rs, :]) + _dot(tri, mid[rs, :]) + _dot(tri, lo[rs, :])

    col_sub = lax.broadcasted_iota(jnp.int32, (SUBLANES, CHUNK), 1)

    def chunk_loop(exact_diag):
        for c in range(t_blk // CHUNK):
            rows = pl.ds(c * CHUNK, CHUNK)
            q = q_ref[rows, :] * (GLA_HK ** -0.5)
            k = k_ref[rows, :]
            v = v_ref[rows, :]
            b = bc_ref[rows, :] * LOG2E
            b_last = b[CHUNK - 1:CHUNK, :]

            st = st_ref[...]
            o = _dot_nt((q * jnp.exp2(b)).astype(BF16), st.astype(BF16))
            kd = (k * jnp.exp2(b_last - b)).astype(BF16)
            st_ref[...] = st * jnp.exp2(b_last) + _dot_tn(v, kd)

            parts = []
            for i in range(n_sub):
                rs = slice(i * SUB, (i + 1) * SUB)
                n_keys = (i if exact_diag else i + 1) * SUB
                if n_keys == 0:
                    parts.append(jnp.zeros((SUB, CHUNK), F32))
                    continue
                if i == 0:
                    qi = q[rs, :] * jnp.exp2(b[rs, :])
                    ki = k[:n_keys, :] * jnp.exp2(-b[:n_keys, :])
                else:
                    b_start = b[i * SUB - 1:i * SUB, :]
                    qi = q[rs, :] * jnp.exp2(b[rs, :] - b_start)
                    ki = k[:n_keys, :] * jnp.exp2(b_start - b[:n_keys, :])
                ki = ki.astype(BF16)
                if n_keys < CHUNK:
                    ki = jnp.concatenate([ki, jnp.zeros((CHUNK - n_keys, GLA_HK), BF16)], axis=0)
                parts.append(_dot_nt(qi.astype(BF16), ki))
            p = jnp.concatenate(parts, axis=0)

            if exact_diag:
                diag = []
                for i in range(n_sub):
                    rs = slice(i * SUB, (i + 1) * SUB)
                    qs, ks, bs = q[rs, :], k[rs, :], b[rs, :]
                    q_hi, b_hi = qs[SUBLANES:, :], bs[SUBLANES:, :]
                    p_lo = jnp.zeros((SUBLANES, CHUNK), F32)
                    p_hi = jnp.zeros((SUBLANES, CHUNK), F32)
                    for jl in range(SUB):
                        kj = ks[jl:jl + 1, :]
                        bj = bs[jl:jl + 1, :]
                        hit = col_sub == i * SUB + jl
                        if jl < SUBLANES:
                            s = jnp.sum(qs * kj * jnp.exp2(bs - bj), axis=-1, keepdims=True)
                            p_lo = jnp.where(hit, s[:SUBLANES, :], p_lo)
                            p_hi = jnp.where(hit, s[SUBLANES:, :], p_hi)
                        else:
                            s = jnp.sum(q_hi * kj * jnp.exp2(b_hi - bj), axis=-1, keepdims=True)
                            p_hi = jnp.where(hit, s, p_hi)
                    diag += [p_lo, p_hi]
                p = p + jnp.where(causal, jnp.concatenate(diag, axis=0), 0.0)
            else:
                p = jnp.where(causal, p, 0.0)
            o = o + _dot(p.astype(BF16), v)

            o = o * lax.rsqrt(jnp.mean(o * o, axis=-1, keepdims=True) + EPS) * gn_ref[...]
            o_ref[rows, :] = (o * sg_ref[rows, :].astype(F32)).astype(BF16)

    mild = jnp.min(la) >= -GLA_MILD_LOG_DECAY

    @pl.when(mild)
    def _():
        chunk_loop(exact_diag=False)

    @pl.when(jnp.logical_not(mild))
    def _():
        chunk_loop(exact_diag=True)


def _gla(qk, v, sg, r, w2, gate_b, gn, *, batch, seq, t_blk=1024):
    t_blk = min(t_blk, seq)
    nt = seq // t_blk
    rowmap = lambda b, h, t: b * nt + t
    return pl.pallas_call(
        functools.partial(_gla_kernel, t_blk=t_blk),
        out_shape=jax.ShapeDtypeStruct((batch * seq, GLA_DV), BF16),
        grid=(batch, GLA_HEADS, nt),
        in_specs=[
            pl.BlockSpec((t_blk, GLA_HK), lambda b, h, t: (rowmap(b, h, t), h)),
            pl.BlockSpec((t_blk, GLA_HK), lambda b, h, t: (rowmap(b, h, t), GLA_HEADS + h)),
            pl.BlockSpec((t_blk, GLA_HV), lambda b, h, t: (rowmap(b, h, t), h)),
            pl.BlockSpec((t_blk, GLA_HV), lambda b, h, t: (rowmap(b, h, t), h)),
            pl.BlockSpec((t_blk, R_PAD), lambda b, h, t: (rowmap(b, h, t), 0)),
            pl.BlockSpec((R_PAD, GLA_HK), lambda b, h, t: (0, h)),
            pl.BlockSpec((1, GLA_HK), lambda b, h, t: (0, h)),
            pl.BlockSpec((1, GLA_HV), lambda b, h, t: (0, h)),
        ],
        out_specs=pl.BlockSpec((t_blk, GLA_HV), lambda b, h, t: (rowmap(b, h, t), h)),
        scratch_shapes=[pltpu.VMEM((GLA_HV, GLA_HK), F32), pltpu.VMEM((t_blk, GLA_HK), F32)],
        compiler_params=_params(("parallel", "parallel", "arbitrary")),
        name="gla",
    )(qk, qk, v, sg, r, w2, gate_b, gn)


def _conv_kernel(c_ref, cw_ref, cb_ref, o_ref, cext_ref, xs_ref, *, tc, cblk, rblk):
    t = pl.program_id(1)

    @pl.when(t == 0)
    def _():
        cext_ref[0:CONV_HALO, :] = jnp.zeros((CONV_HALO, D_MODEL), F32)

    @pl.when(t > 0)
    def _():
        cext_ref[0:CONV_HALO, :] = cext_ref[tc:tc + CONV_HALO, :]

    cext_ref[CONV_HALO:CONV_HALO + tc, :] = c_ref[...]

    row0 = CONV_HALO - (CONV_K - 1)
    n_shift = tc + CONV_HALO - SUBLANES

    def col_body(cb, carry):
        cols = pl.ds(pl.multiple_of(cb * cblk, cblk), cblk)
        for s in range(1, SUBLANES):
            xs_ref[s, 0:n_shift, :] = cext_ref[pl.ds(s, n_shift), cols]
        w = cw_ref[:, cols]
        bias = cb_ref[:, cols]
        for rb in range(tc // rblk):
            acc = jnp.broadcast_to(bias, (rblk, cblk))
            for kk in range(CONV_K):
                s = (row0 + kk) % SUBLANES
                base = rb * rblk + (row0 + kk) - s
                if s == 0:
                    win = cext_ref[pl.ds(base, rblk), cols]
                else:
                    win = xs_ref[s, pl.ds(base, rblk), :]
                acc = acc + w[kk:kk + 1, :] * win
            o_ref[pl.ds(rb * rblk, rblk), cols] = acc
        return carry

    lax.fori_loop(0, D_MODEL // cblk, col_body, 0)


def _conv(c, conv_w, conv_b, *, batch, seq, tc=256, cblk=256, rblk=64):
    nt = seq // tc
    return pl.pallas_call(
        functools.partial(_conv_kernel, tc=tc, cblk=cblk, rblk=rblk),
        out_shape=jax.ShapeDtypeStruct((batch * seq, D_MODEL), F32),
        grid=(batch, nt),
        in_specs=[
            pl.BlockSpec((tc, D_MODEL), lambda b, t: (b * nt + t, 0)),
            pl.BlockSpec((CONV_K, D_MODEL), lambda b, t: (0, 0)),
            pl.BlockSpec((1, D_MODEL), lambda b, t: (0, 0)),
        ],
        out_specs=pl.BlockSpec((tc, D_MODEL), lambda b, t: (b * nt + t, 0)),
        scratch_shapes=[pltpu.VMEM((tc + CONV_HALO, D_MODEL), F32),
                        pltpu.VMEM((SUBLANES, tc + CONV_HALO, cblk), F32)],
        compiler_params=_params(("parallel", "arbitrary")),
        name="conv",
    )(c, conv_w, conv_b)


def _merge_kernel(x_ref, og_ref, cv_ref, lg_ref, lb_ref, g0_ref, g1_ref, wg_ref, wc_ref, wo_ref,
                  o_ref):
    ya = _dot(og_ref[...], wg_ref[...])
    c = cv_ref[...]
    xc = c - jnp.mean(c, axis=-1, keepdims=True)
    y = xc * lax.rsqrt(jnp.mean(xc * xc, axis=-1, keepdims=True) + EPS) * lg_ref[...] + lb_ref[...]
    yb = _dot(_silu(y).astype(BF16), wc_ref[...])
    m = g0_ref[...].astype(F32) * ya + g1_ref[...].astype(F32) * yb
    o_ref[...] = x_ref[...] + _dot(m.astype(BF16), wo_ref[...])


def _merge(x, og, cv, ln_g, ln_b, gates, w_gla, w_conv, w_out, *, tm=256):
    n = x.shape[0]
    row = lambda i: (i, 0)
    const = lambda i: (0, 0)
    wspec = _resident((D_MODEL, D_MODEL), const)
    return pl.pallas_call(
        _merge_kernel,
        out_shape=jax.ShapeDtypeStruct((n, D_MODEL), F32),
        grid=(n // tm,),
        in_specs=[
            pl.BlockSpec((tm, D_MODEL), row),
            pl.BlockSpec((tm, D_MODEL), row),
            pl.BlockSpec((tm, D_MODEL), row),
            pl.BlockSpec((1, D_MODEL), const),
            pl.BlockSpec((1, D_MODEL), const),
            pl.BlockSpec((tm, D_MODEL), lambda i: (i, 0)),
            pl.BlockSpec((tm, D_MODEL), lambda i: (i, 1)),
            wspec, wspec, wspec,
        ],
        out_specs=pl.BlockSpec((tm, D_MODEL), row),
        compiler_params=_params(("parallel",)),
        name="merge",
    )(x, og, cv, ln_g, ln_b, gates, gates, w_gla, w_conv, w_out)


def _xattn_kernel(x_ref, g_ref, wq_ref, k_ref, v_ref, wo_ref, o_ref):
    x = x_ref[...]
    h = _rms(x, g_ref[...]).astype(BF16)
    q = _dot(h, wq_ref[...]).astype(BF16)
    k = k_ref[...].astype(BF16)
    v = v_ref[...].astype(BF16)
    outs = []
    for hd in range(XA_HEADS):
        cs = slice(hd * XA_HEAD_DIM, (hd + 1) * XA_HEAD_DIM)
        s = _dot_nt(q[:, cs], k[:, cs]) * (XA_HEAD_DIM ** -0.5)
        e = jnp.exp(s - jnp.max(s, axis=-1, keepdims=True))
        p = e / jnp.sum(e, axis=-1, keepdims=True)
        outs.append(_dot(p.astype(BF16), v[:, cs]))
    o = jnp.concatenate(outs, axis=-1).astype(BF16)
    o_ref[...] = x + _dot(o, wo_ref[...])


def _xattn(x, kv, norm_g, w_q, w_out, *, batch, seq, n_mem, tm=512):
    nt = seq // tm
    return pl.pallas_call(
        _xattn_kernel,
        out_shape=jax.ShapeDtypeStruct((batch * seq, D_MODEL), F32),
        grid=(batch, nt),
        in_specs=[
            pl.BlockSpec((tm, D_MODEL), lambda b, t: (b * nt + t, 0)),
            pl.BlockSpec((1, D_MODEL), lambda b, t: (0, 0)),
            pl.BlockSpec((D_MODEL, XA_WIDTH), lambda b, t: (0, 0)),
            pl.BlockSpec((n_mem, XA_WIDTH), lambda b, t: (b, 0)),
            pl.BlockSpec((n_mem, XA_WIDTH), lambda b, t: (b, 1)),
            pl.BlockSpec((XA_WIDTH, D_MODEL), lambda b, t: (0, 0)),
        ],
        out_specs=pl.BlockSpec((tm, D_MODEL), lambda b, t: (b * nt + t, 0)),
        compiler_params=_params(("parallel", "parallel")),
        name="xattn",
    )(x, norm_g, w_q, kv, kv, w_out)


def _row(v):
    return v.reshape(1, -1)


def kernel(x, mem, ffn1_norm, ffn1_w_in, ffn1_w_out, mix_norm, mix_w_in, gla_gate_w2, gla_gate_b, gla_out_norm, gla_proj, conv_w, conv_b, conv_ln_g, conv_ln_b, conv_proj, branch_gate_b, mix_w_out, xa_norm, xa_mem_norm, xa_w_q, xa_w_kv, xa_w_out, ffn2_norm, ffn2_w_in, ffn2_w_out, final_norm):
    batch, seq, _ = x.shape
    n_mem = mem.shape[1]
    depth = ffn1_norm.shape[0]
    xs = x.reshape(batch * seq, D_MODEL)
    mems = mem.reshape(batch * n_mem, D_MODEL)
    r0 = 2 * GLA_DK + GLA_DV
    r1 = r0 + GLA_RANK
    fin = _row(final_norm)

    for l in range(depth):
        w_mix = mix_w_in[l]
        w_z = jnp.concatenate([w_mix[:, :r0], w_mix[:, r1:]], axis=1).astype(BF16)
        w_r = jnp.pad(w_mix[:, r0:r1], ((0, 0), (0, R_PAD - GLA_RANK))).astype(BF16)
        w2 = jnp.pad(gla_gate_w2[l], ((0, R_PAD - GLA_RANK), (0, 0))).astype(BF16)

        xs = _ffn(xs, _row(ffn1_norm[l]), _cast_layer(ffn1_w_in, l), _cast_layer(ffn1_w_out, l),
                  fin, final_norm=False)

        qk, v, sg, c, gates, r = _proj(xs, _row(mix_norm[l]), w_z, w_r, _row(branch_gate_b[l]))
        og = _gla(qk, v, sg, r, w2, _row(gla_gate_b[l]), _row(gla_out_norm[l]),
                  batch=batch, seq=seq)
        cv = _conv(c, conv_w[l], _row(conv_b[l]), batch=batch, seq=seq)
        xs = _merge(xs, og, cv, _row(conv_ln_g[l]), _row(conv_ln_b[l]), gates,
                    _cast_layer(gla_proj, l), _cast_layer(conv_proj, l), _cast_layer(mix_w_out, l))

        kv = _norm_matmul(mems, _row(xa_mem_norm[l]), _cast_layer(xa_w_kv, l), tm=batch * n_mem)
        xs = _xattn(xs, kv, _row(xa_norm[l]), _cast_layer(xa_w_q, l), _cast_layer(xa_w_out, l),
                    batch=batch, seq=seq, n_mem=n_mem)

        xs = _ffn(xs, _row(ffn2_norm[l]), _cast_layer(ffn2_w_in, l), _cast_layer(ffn2_w_out, l),
                  fin, final_norm=(l == depth - 1))

    return xs.reshape(batch, seq, D_MODEL)
```

```python
import functools
import math

import jax
import jax.numpy as jnp
from jax import lax
from jax.experimental import pallas as pl
from jax.experimental.pallas import tpu as pltpu

F32 = jnp.float32
BF16 = jnp.bfloat16

D_MODEL = 2048
CHUNK = 64
SUB = 16
SUBLANES = 8
GLA_HEADS = 4
GLA_DK = D_MODEL // 2
GLA_DV = D_MODEL
GLA_HK = GLA_DK // GLA_HEADS
GLA_HV = GLA_DV // GLA_HEADS
GLA_RANK = 16
GLA_TAU = 16.0
GLA_MILD_LOG_DECAY = 4.0
CONV_K = 31
CONV_HALO = 32
XA_HEADS = 4
XA_HEAD_DIM = 128
XA_WIDTH = XA_HEADS * XA_HEAD_DIM
D_FF = 5632
FFN_RES = 0.5
EPS = 1e-6
LANE = 128
R_PAD = LANE
LOG2E = math.log2(math.e)

VMEM_LIMIT = 56 * 1024 * 1024


def _params(sem):
    return pltpu.CompilerParams(dimension_semantics=sem, vmem_limit_bytes=VMEM_LIMIT)


def _rms(x, g):
    return x * lax.rsqrt(jnp.mean(x * x, axis=-1, keepdims=True) + EPS) * g


def _silu(x):
    return x * jax.nn.sigmoid(x)


def _dot(a, b):
    return jnp.dot(a, b, preferred_element_type=F32)


def _dot_nt(a, b):
    return lax.dot_general(a, b, (((1,), (1,)), ((), ())), preferred_element_type=F32)


def _dot_tn(a, b):
    return lax.dot_general(a, b, (((0,), (0,)), ((), ())), preferred_element_type=F32)


def _resident(shape, index_map):
    return pl.BlockSpec(shape, index_map, pipeline_mode=pl.Buffered(1))


def _cast_kernel(w_ref, o_ref):
    o_ref[...] = w_ref[...].astype(BF16)


def _cast_layer(w, l, *, block_elems=1 << 21):
    _, r, c = w.shape
    tr = min(r, 1 << int(math.log2(block_elems // c)))
    while r % tr:
        tr //= 2
    return pl.pallas_call(
        _cast_kernel,
        out_shape=jax.ShapeDtypeStruct((r, c), BF16),
        grid=(r // tr,),
        in_specs=[pl.BlockSpec((None, tr, c), lambda i: (l, i, 0))],
        out_specs=pl.BlockSpec((tr, c), lambda i: (i, 0)),
        compiler_params=_params(("parallel",)),
        name="cast",
    )(w)


def _ffn_kernel(x_ref, g_ref, wa_ref, wb_ref, w2_ref, fg_ref, o_ref, h_ref, *, final_norm):
    j = pl.program_id(1)

    @pl.when(j == 0)
    def _():
        x = x_ref[...]
        h_ref[...] = _rms(x, g_ref[...]).astype(BF16)
        o_ref[...] = x

    h = h_ref[...]
    a = _dot(h, wa_ref[...])
    b = _dot(h, wb_ref[...])
    g = (_silu(a) * b * FFN_RES).astype(BF16)
    o_ref[...] += _dot(g, w2_ref[...])

    if final_norm:
        @pl.when(j == pl.num_programs(1) - 1)
        def _():
            o_ref[...] = _rms(o_ref[...], fg_ref[...])


def _ffn(x, norm_g, w_in, w_out, final_g, *, final_norm, tm=1024, tf=512):
    n = x.shape[0]
    nf = D_FF // tf
    return pl.pallas_call(
        functools.partial(_ffn_kernel, final_norm=final_norm),
        out_shape=jax.ShapeDtypeStruct((n, D_MODEL), F32),
        grid=(n // tm, nf),
        in_specs=[
            pl.BlockSpec((tm, D_MODEL), lambda i, j: (i, 0)),
            pl.BlockSpec((1, D_MODEL), lambda i, j: (0, 0)),
            pl.BlockSpec((D_MODEL, tf), lambda i, j: (0, j)),
            pl.BlockSpec((D_MODEL, tf), lambda i, j: (0, j + nf)),
            pl.BlockSpec((tf, D_MODEL), lambda i, j: (j, 0)),
            pl.BlockSpec((1, D_MODEL), lambda i, j: (0, 0)),
        ],
        out_specs=pl.BlockSpec((tm, D_MODEL), lambda i, j: (i, 0)),
        scratch_shapes=[pltpu.VMEM((tm, D_MODEL), BF16)],
        compiler_params=_params(("parallel", "arbitrary")),
        name="ffn",
    )(x, norm_g, w_in, w_in, w_out, final_g)


def _proj_kernel(x_ref, g_ref, wa_ref, wb_ref, wr_ref, gb_ref,
                 qk_ref, v_ref, sg_ref, c_ref, gt_ref, r_ref, h_ref, *, bounds):
    j = pl.program_id(1)
    e_qk, e_v, e_g, e_c = bounds

    @pl.when(j == 0)
    def _():
        h = _rms(x_ref[...], g_ref[...]).astype(BF16)
        h_ref[...] = h
        r_ref[...] = _dot(h, wr_ref[...])

    @pl.when(j < e_qk)
    def _():
        qk_ref[...] = _dot(h_ref[...], wa_ref[...])

    @pl.when((j >= e_qk) & (j < e_v))
    def _():
        v_ref[...] = _dot(h_ref[...], wa_ref[...]).astype(BF16)

    half = h_ref.shape[0] // 2
    halves = (pl.ds(0, half), pl.ds(half, half))

    @pl.when((j >= e_v) & (j < e_g))
    def _():
        for rs in halves:
            sg_ref[rs, :] = _silu(_dot(h_ref[rs, :], wa_ref[...])).astype(BF16)

    @pl.when((j >= e_g) & (j < e_c))
    def _():
        for rs in halves:
            h = h_ref[rs, :]
            c_ref[rs, :] = _dot(h, wa_ref[...]) * jax.nn.sigmoid(_dot(h, wb_ref[...]))

    @pl.when(j >= e_c)
    def _():
        for rs in halves:
            gt_ref[rs, :] = jax.nn.sigmoid(_dot(h_ref[rs, :], wa_ref[...])
                                           + gb_ref[...]).astype(BF16)


def _proj(x, norm_g, w_z, w_r, gate_b, *, tm=1024, tn=512):
    n = x.shape[0]
    n_qk, n_v, n_g, n_c, n_gt = (2 * GLA_DK // tn, GLA_DV // tn, GLA_DV // tn, D_MODEL // tn,
                                 2 * D_MODEL // tn)
    e_qk = n_qk
    e_v = e_qk + n_v
    e_g = e_v + n_g
    e_c = e_g + n_c
    steps = e_c + n_gt

    def park(j, lo, cnt):
        return jnp.clip(j - lo, 0, cnt - 1)

    return pl.pallas_call(
        functools.partial(_proj_kernel, bounds=(e_qk, e_v, e_g, e_c)),
        out_shape=(jax.ShapeDtypeStruct((n, 2 * GLA_DK), F32),
                   jax.ShapeDtypeStruct((n, GLA_DV), BF16),
                   jax.ShapeDtypeStruct((n, GLA_DV), BF16),
                   jax.ShapeDtypeStruct((n, D_MODEL), F32),
                   jax.ShapeDtypeStruct((n, 2 * D_MODEL), BF16),
                   jax.ShapeDtypeStruct((n, R_PAD), F32)),
        grid=(n // tm, steps),
        in_specs=[
            pl.BlockSpec((tm, D_MODEL), lambda i, j: (i, 0)),
            pl.BlockSpec((1, D_MODEL), lambda i, j: (0, 0)),
            pl.BlockSpec((D_MODEL, tn), lambda i, j: (0, jnp.where(j < e_c, j, j + n_c))),
            pl.BlockSpec((D_MODEL, tn), lambda i, j: (0, e_c + park(j, e_g, n_c))),
            pl.BlockSpec((D_MODEL, R_PAD), lambda i, j: (0, 0)),
            pl.BlockSpec((1, tn), lambda i, j: (0, park(j, e_c, n_gt))),
        ],
        out_specs=(pl.BlockSpec((tm, tn), lambda i, j: (i, park(j, 0, n_qk))),
                   pl.BlockSpec((tm, tn), lambda i, j: (i, park(j, e_qk, n_v))),
                   pl.BlockSpec((tm, tn), lambda i, j: (i, park(j, e_v, n_g))),
                   pl.BlockSpec((tm, tn), lambda i, j: (i, park(j, e_g, n_c))),
                   pl.BlockSpec((tm, tn), lambda i, j: (i, park(j, e_c, n_gt))),
                   pl.BlockSpec((tm, R_PAD), lambda i, j: (i, 0))),
        scratch_shapes=[pltpu.VMEM((tm, D_MODEL), BF16)],
        compiler_params=_params(("parallel", "arbitrary")),
        name="proj",
    )(x, norm_g, w_z, w_z, w_r, gate_b)


def _norm_matmul_kernel(x_ref, g_ref, w_ref, o_ref):
    h = _rms(x_ref[...], g_ref[...]).astype(BF16)
    o_ref[...] = _dot(h, w_ref[...])


def _norm_matmul(x, norm_g, w, *, tm):
    n = x.shape[0]
    nout = w.shape[1]
    return pl.pallas_call(
        _norm_matmul_kernel,
        out_shape=jax.ShapeDtypeStruct((n, nout), F32),
        grid=(n // tm,),
        in_specs=[
            pl.BlockSpec((tm, D_MODEL), lambda i: (i, 0)),
            pl.BlockSpec((1, D_MODEL), lambda i: (0, 0)),
            pl.BlockSpec((D_MODEL, nout), lambda i: (0, 0)),
        ],
        out_specs=pl.BlockSpec((tm, nout), lambda i: (i, 0)),
        compiler_params=_params(("parallel",)),
        name="norm_matmul",
    )(x, norm_g, w)


def _split3(x):
    hi = x.astype(BF16)
    r1 = x - hi.astype(F32)
    mid = r1.astype(BF16)
    lo = (r1 - mid.astype(F32)).astype(BF16)
    return hi, mid, lo


def _gla_kernel(q_ref, k_ref, v_ref, sg_ref, r_ref, w2_ref, gb_ref, gn_ref, o_ref, st_ref, bc_ref,
                *, t_blk):
    @pl.when(pl.program_id(2) == 0)
    def _():
        st_ref[...] = jnp.zeros_like(st_ref)

    n_sub = CHUNK // SUB
    row = lax.broadcasted_iota(jnp.int32, (CHUNK, CHUNK), 0)
    col = lax.broadcasted_iota(jnp.int32, (CHUNK, CHUNK), 1)
    tri = (row >= col).astype(BF16)
    causal = col <= row

    pre = _dot(r_ref[...].astype(BF16), w2_ref[...]) + gb_ref[...]
    la = (jnp.minimum(pre, 0.0) - jnp.log1p(jnp.exp(-jnp.abs(pre)))) * (1.0 / GLA_TAU)
    hi, mid, lo = _split3(la)
    for c in range(t_blk // CHUNK):
        rs = slice(c * CHUNK, (c + 1) * CHUNK)
        bc_ref[rs, :] = _dot(tri, hi[rs, :]) + _dot(tri, mid[rs, :]) + _dot(tri, lo[rs, :])

    col_sub = lax.broadcasted_iota(jnp.int32, (SUBLANES, CHUNK), 1)

    def chunk_loop(exact_diag):
        for c in range(t_blk // CHUNK):
            rows = pl.ds(c * CHUNK, CHUNK)
            q = q_ref[rows, :] * (GLA_HK ** -0.5)
            k = k_ref[rows, :]
            v = v_ref[rows, :]
            b = bc_ref[rows, :] * LOG2E
            b_last = b[CHUNK - 1:CHUNK, :]

            st = st_ref[...]
            o = _dot_nt((q * jnp.exp2(b)).astype(BF16), st.astype(BF16))
            kd = (k * jnp.exp2(b_last - b)).astype(BF16)
            st_ref[...] = st * jnp.exp2(b_last) + _dot_tn(v, kd)

            parts = []
            for i in range(n_sub):
                rs = slice(i * SUB, (i + 1) * SUB)
                n_keys = (i if exact_diag else i + 1) * SUB
                if n_keys == 0:
                    parts.append(jnp.zeros((SUB, CHUNK), F32))
                    continue
                if i == 0:
                    qi = q[rs, :] * jnp.exp2(b[rs, :])
                    ki = k[:n_keys, :] * jnp.exp2(-b[:n_keys, :])
                else:
                    b_start = b[i * SUB - 1:i * SUB, :]
                    qi = q[rs, :] * jnp.exp2(b[rs, :] - b_start)
                    ki = k[:n_keys, :] * jnp.exp2(b_start - b[:n_keys, :])
                ki = ki.astype(BF16)
                if n_keys < CHUNK:
                    ki = jnp.concatenate([ki, jnp.zeros((CHUNK - n_keys, GLA_HK), BF16)], axis=0)
                parts.append(_dot_nt(qi.astype(BF16), ki))
            p = jnp.concatenate(parts, axis=0)

            if exact_diag:
                diag = []
                for i in range(n_sub):
                    rs = slice(i * SUB, (i + 1) * SUB)
                    qs, ks, bs = q[rs, :], k[rs, :], b[rs, :]
                    q_hi, b_hi = qs[SUBLANES:, :], bs[SUBLANES:, :]
                    p_lo = jnp.zeros((SUBLANES, CHUNK), F32)
                    p_hi = jnp.zeros((SUBLANES, CHUNK), F32)
                    for jl in range(SUB):
                        kj = ks[jl:jl + 1, :]
                        bj = bs[jl:jl + 1, :]
                        hit = col_sub == i * SUB + jl
                        if jl < SUBLANES:
                            s = jnp.sum(qs * kj * jnp.exp2(bs - bj), axis=-1, keepdims=True)
                            p_lo = jnp.where(hit, s[:SUBLANES, :], p_lo)
                            p_hi = jnp.where(hit, s[SUBLANES:, :], p_hi)
                        else:
                            s = jnp.sum(q_hi * kj * jnp.exp2(b_hi - bj), axis=-1, keepdims=True)
                            p_hi = jnp.where(hit, s, p_hi)
                    diag += [p_lo, p_hi]
                p = p + jnp.where(causal, jnp.concatenate(diag, axis=0), 0.0)
            else:
                p = jnp.where(causal, p, 0.0)
            o = o + _dot(p.astype(BF16), v)

            o = o * lax.rsqrt(jnp.mean(o * o, axis=-1, keepdims=True) + EPS) * gn_ref[...]
            o_ref[rows, :] = (o * sg_ref[rows, :].astype(F32)).astype(BF16)

    mild = jnp.min(la) >= -GLA_MILD_LOG_DECAY

    @pl.when(mild)
    def _():
        chunk_loop(exact_diag=False)

    @pl.when(jnp.logical_not(mild))
    def _():
        chunk_loop(exact_diag=True)


def _gla(qk, v, sg, r, w2, gate_b, gn, *, batch, seq, t_blk=1024):
    t_blk = min(t_blk, seq)
    nt = seq // t_blk
    rowmap = lambda b, h, t: b * nt + t
    return pl.pallas_call(
        functools.partial(_gla_kernel, t_blk=t_blk),
        out_shape=jax.ShapeDtypeStruct((batch * seq, GLA_DV), BF16),
        grid=(batch, GLA_HEADS, nt),
        in_specs=[
            pl.BlockSpec((t_blk, GLA_HK), lambda b, h, t: (rowmap(b, h, t), h)),
            pl.BlockSpec((t_blk, GLA_HK), lambda b, h, t: (rowmap(b, h, t), GLA_HEADS + h)),
            pl.BlockSpec((t_blk, GLA_HV), lambda b, h, t: (rowmap(b, h, t), h)),
            pl.BlockSpec((t_blk, GLA_HV), lambda b, h, t: (rowmap(b, h, t), h)),
            pl.BlockSpec((t_blk, R_PAD), lambda b, h, t: (rowmap(b, h, t), 0)),
            pl.BlockSpec((R_PAD, GLA_HK), lambda b, h, t: (0, h)),
            pl.BlockSpec((1, GLA_HK), lambda b, h, t: (0, h)),
            pl.BlockSpec((1, GLA_HV), lambda b, h, t: (0, h)),
        ],
        out_specs=pl.BlockSpec((t_blk, GLA_HV), lambda b, h, t: (rowmap(b, h, t), h)),
        scratch_shapes=[pltpu.VMEM((GLA_HV, GLA_HK), F32), pltpu.VMEM((t_blk, GLA_HK), F32)],
        compiler_params=_params(("parallel", "parallel", "arbitrary")),
        name="gla",
    )(qk, qk, v, sg, r, w2, gate_b, gn)


def _conv_kernel(c_ref, cw_ref, cb_ref, o_ref, cext_ref, xs_ref, *, tc, cblk, rblk):
    t = pl.program_id(1)

    @pl.when(t == 0)
    def _():
        cext_ref[0:CONV_HALO, :] = jnp.zeros((CONV_HALO, D_MODEL), F32)

    @pl.when(t > 0)
    def _():
        cext_ref[0:CONV_HALO, :] = cext_ref[tc:tc + CONV_HALO, :]

    cext_ref[CONV_HALO:CONV_HALO + tc, :] = c_ref[...]

    row0 = CONV_HALO - (CONV_K - 1)
    n_shift = tc + CONV_HALO - SUBLANES

    def col_body(cb, carry):
        cols = pl.ds(pl.multiple_of(cb * cblk, cblk), cblk)
        for s in range(1, SUBLANES):
            xs_ref[s, 0:n_shift, :] = cext_ref[pl.ds(s, n_shift), cols]
        w = cw_ref[:, cols]
        bias = cb_ref[:, cols]
        for rb in range(tc // rblk):
            acc = jnp.broadcast_to(bias, (rblk, cblk))
            for kk in range(CONV_K):
                s = (row0 + kk) % SUBLANES
                base = rb * rblk + (row0 + kk) - s
                if s == 0:
                    win = cext_ref[pl.ds(base, rblk), cols]
                else:
                    win = xs_ref[s, pl.ds(base, rblk), :]
                acc = acc + w[kk:kk + 1, :] * win
            o_ref[pl.ds(rb * rblk, rblk), cols] = acc
        return carry

    lax.fori_loop(0, D_MODEL // cblk, col_body, 0)


def _conv(c, conv_w, conv_b, *, batch, seq, tc=256, cblk=128, rblk=128):
    nt = seq // tc
    return pl.pallas_call(
        functools.partial(_conv_kernel, tc=tc, cblk=cblk, rblk=rblk),
        out_shape=jax.ShapeDtypeStruct((batch * seq, D_MODEL), F32),
        grid=(batch, nt),
        in_specs=[
            pl.BlockSpec((tc, D_MODEL), lambda b, t: (b * nt + t, 0)),
            pl.BlockSpec((CONV_K, D_MODEL), lambda b, t: (0, 0)),
            pl.BlockSpec((1, D_MODEL), lambda b, t: (0, 0)),
        ],
        out_specs=pl.BlockSpec((tc, D_MODEL), lambda b, t: (b * nt + t, 0)),
        scratch_shapes=[pltpu.VMEM((tc + CONV_HALO, D_MODEL), F32),
                        pltpu.VMEM((SUBLANES, tc + CONV_HALO, cblk), F32)],
        compiler_params=_params(("parallel", "arbitrary")),
        name="conv",
    )(c, conv_w, conv_b)


def _merge_kernel(x_ref, og_ref, cv_ref, lg_ref, lb_ref, g0_ref, g1_ref, wg_ref, wc_ref, wo_ref,
                  o_ref):
    ya = _dot(og_ref[...], wg_ref[...])
    c = cv_ref[...]
    xc = c - jnp.mean(c, axis=-1, keepdims=True)
    y = xc * lax.rsqrt(jnp.mean(xc * xc, axis=-1, keepdims=True) + EPS) * lg_ref[...] + lb_ref[...]
    yb = _dot(_silu(y).astype(BF16), wc_ref[...])
    m = g0_ref[...].astype(F32) * ya + g1_ref[...].astype(F32) * yb
    o_ref[...] = x_ref[...] + _dot(m.astype(BF16), wo_ref[...])


def _merge(x, og, cv, ln_g, ln_b, gates, w_gla, w_conv, w_out, *, tm=256):
    n = x.shape[0]
    row = lambda i: (i, 0)
    const = lambda i: (0, 0)
    wspec = _resident((D_MODEL, D_MODEL), const)
    return pl.pallas_call(
        _merge_kernel,
        out_shape=jax.ShapeDtypeStruct((n, D_MODEL), F32),
        grid=(n // tm,),
        in_specs=[
            pl.BlockSpec((tm, D_MODEL), row),
            pl.BlockSpec((tm, D_MODEL), row),
            pl.BlockSpec((tm, D_MODEL), row),
            pl.BlockSpec((1, D_MODEL), const),
            pl.BlockSpec((1, D_MODEL), const),
            pl.BlockSpec((tm, D_MODEL), lambda i: (i, 0)),
            pl.BlockSpec((tm, D_MODEL), lambda i: (i, 1)),
            wspec, wspec, wspec,
        ],
        out_specs=pl.BlockSpec((tm, D_MODEL), row),
        compiler_params=_params(("parallel",)),
        name="merge",
    )(x, og, cv, ln_g, ln_b, gates, gates, w_gla, w_conv, w_out)


def _xattn_kernel(x_ref, g_ref, wq_ref, k_ref, v_ref, wo_ref, o_ref):
    x = x_ref[...]
    h = _rms(x, g_ref[...]).astype(BF16)
    q = _dot(h, wq_ref[...]).astype(BF16)
    k = k_ref[...].astype(BF16)
    v = v_ref[...].astype(BF16)
    outs = []
    for hd in range(XA_HEADS):
        cs = slice(hd * XA_HEAD_DIM, (hd + 1) * XA_HEAD_DIM)
        s = _dot_nt(q[:, cs], k[:, cs]) * (XA_HEAD_DIM ** -0.5)
        e = jnp.exp(s - jnp.max(s, axis=-1, keepdims=True))
        p = e / jnp.sum(e, axis=-1, keepdims=True)
        outs.append(_dot(p.astype(BF16), v[:, cs]))
    o = jnp.concatenate(outs, axis=-1).astype(BF16)
    o_ref[...] = x + _dot(o, wo_ref[...])


def _xattn(x, kv, norm_g, w_q, w_out, *, batch, seq, n_mem, tm=512):
    nt = seq // tm
    return pl.pallas_call(
        _xattn_kernel,
        out_shape=jax.ShapeDtypeStruct((batch * seq, D_MODEL), F32),
        grid=(batch, nt),
        in_specs=[
            pl.BlockSpec((tm, D_MODEL), lambda b, t: (b * nt + t, 0)),
            pl.BlockSpec((1, D_MODEL), lambda b, t: (0, 0)),
            pl.BlockSpec((D_MODEL, XA_WIDTH), lambda b, t: (0, 0)),
            pl.BlockSpec((n_mem, XA_WIDTH), lambda b, t: (b, 0)),
            pl.BlockSpec((n_mem, XA_WIDTH), lambda b, t: (b, 1)),
            pl.BlockSpec((XA_WIDTH, D_MODEL), lambda b, t: (0, 0)),
        ],
        out_specs=pl.BlockSpec((tm, D_MODEL), lambda b, t: (b * nt + t, 0)),
        compiler_params=_params(("parallel", "parallel")),
        name="xattn",
    )(x, norm_g, w_q, kv, kv, w_out)


def _row(v):
    return v.reshape(1, -1)


def kernel(x, mem, ffn1_norm, ffn1_w_in, ffn1_w_out, mix_norm, mix_w_in, gla_gate_w2, gla_gate_b, gla_out_norm, gla_proj, conv_w, conv_b, conv_ln_g, conv_ln_b, conv_proj, branch_gate_b, mix_w_out, xa_norm, xa_mem_norm, xa_w_q, xa_w_kv, xa_w_out, ffn2_norm, ffn2_w_in, ffn2_w_out, final_norm):
    batch, seq, _ = x.shape
    n_mem = mem.shape[1]
    depth = ffn1_norm.shape[0]
    xs = x.reshape(batch * seq, D_MODEL)
    mems = mem.reshape(batch * n_mem, D_MODEL)
    r0 = 2 * GLA_DK + GLA_DV
    r1 = r0 + GLA_RANK
    fin = _row(final_norm)

    for l in range(depth):
        w_mix = mix_w_in[l]
        w_z = jnp.concatenate([w_mix[:, :r0], w_mix[:, r1:]], axis=1).astype(BF16)
        w_r = jnp.pad(w_mix[:, r0:r1], ((0, 0), (0, R_PAD - GLA_RANK))).astype(BF16)
        w2 = jnp.pad(gla_gate_w2[l], ((0, R_PAD - GLA_RANK), (0, 0))).astype(BF16)

        xs = _ffn(xs, _row(ffn1_norm[l]), _cast_layer(ffn1_w_in, l), _cast_layer(ffn1_w_out, l),
                  fin, final_norm=False)

        qk, v, sg, c, gates, r = _proj(xs, _row(mix_norm[l]), w_z, w_r, _row(branch_gate_b[l]))
        og = _gla(qk, v, sg, r, w2, _row(gla_gate_b[l]), _row(gla_out_norm[l]),
                  batch=batch, seq=seq)
        cv = _conv(c, conv_w[l], _row(conv_b[l]), batch=batch, seq=seq)
        xs = _merge(xs, og, cv, _row(conv_ln_g[l]), _row(conv_ln_b[l]), gates,
                    _cast_layer(gla_proj, l), _cast_layer(conv_proj, l), _cast_layer(mix_w_out, l))

        kv = _norm_matmul(mems, _row(xa_mem_norm[l]), _cast_layer(xa_w_kv, l), tm=batch * n_mem)
        xs = _xattn(xs, kv, _row(xa_norm[l]), _cast_layer(xa_w_q, l), _cast_layer(xa_w_out, l),
                    batch=batch, seq=seq, n_mem=n_mem)

        xs = _ffn(xs, _row(ffn2_norm[l]), _cast_layer(ffn2_w_in, l), _cast_layer(ffn2_w_out, l),
                  fin, final_norm=(l == depth - 1))

    return xs.reshape(batch, seq, D_MODEL)
```
